```python
import math
import jax, jax.numpy as jnp
from jax import lax
import numpy as np

D_MODEL = 2048
BATCH = 32
SEQ = 256
DEPTH = 4
DEC_BATCH = 2
DEC_SEQ = 2048
PAST_LEN = 512

GRID_W = 64
ATT_HEADS = D_MODEL // 256
ATT_HEAD_DIM = 64
ATT_QK_WIDTH = 2 * ATT_HEADS * ATT_HEAD_DIM
ATT_V_WIDTH = ATT_HEADS * 2 * ATT_HEAD_DIM
ROPE_PAIRS = ATT_HEAD_DIM // 4
ROPE_BASE = 10000.0
Q_BLOCK = 128
SSM_WIDTH = D_MODEL // 2
SSM_GROUP_CH = 16
SSM_GROUPS = SSM_WIDTH // SSM_GROUP_CH
SSM_STATE = 64
CONV_CH = D_MODEL // 2
CONV_K = 31
CONV_PAD = CONV_K // 2
IN_SPLITS = (ATT_QK_WIDTH, 2 * ATT_QK_WIDTH, 2 * ATT_QK_WIDTH + ATT_V_WIDTH,
             2 * ATT_QK_WIDTH + ATT_V_WIDTH + SSM_WIDTH)
IN_WIDTH = 2 * ATT_QK_WIDTH + ATT_V_WIDTH + SSM_WIDTH + 2 * CONV_CH
N_BRANCHES = 3
N_EXPERTS = 32
TOP_K = 4
EXPERT_FF = D_MODEL
SWIGLU_LIMIT = 7.0
SWIGLU_ALPHA = 1.702
MOE_BLOCK = 128
EPS = 1e-6

kernel_name = "hybrid_diffusion_prefix_step"


def rms_norm(x, g):
    xf = x.astype(jnp.float32)
    y = xf * lax.rsqrt(jnp.mean(xf * xf, axis=-1, keepdims=True) + EPS)
    return y.astype(x.dtype) * g


def layer_norm(x, g, b):
    xf = x.astype(jnp.float32)
    mu = jnp.mean(xf, axis=-1, keepdims=True)
    var = jnp.mean(jnp.square(xf - mu), axis=-1, keepdims=True)
    return ((xf - mu) * lax.rsqrt(var + EPS)).astype(x.dtype) * g + b


def axial_rope_tables(n_tok):
    rows = n_tok // GRID_W
    row = jnp.broadcast_to(jnp.arange(rows)[:, None], (rows, GRID_W)).reshape(-1).astype(jnp.float32)
    col = jnp.broadcast_to(jnp.arange(GRID_W)[None, :], (rows, GRID_W)).reshape(-1).astype(jnp.float32)
    inv_freq = ROPE_BASE ** (-jnp.arange(ROPE_PAIRS, dtype=jnp.float32) / ROPE_PAIRS)
    ang = jnp.stack([row[:, None] * inv_freq, col[:, None] * inv_freq], axis=1)
    return jnp.cos(ang), jnp.sin(ang)


def apply_rope(x, cos, sin):
    bsz, n_tok, n_h, _ = x.shape
    xr = x.reshape(bsz, n_tok, n_h, 2, 2, ROPE_PAIRS)
    x1, x2 = xr[..., 0, :], xr[..., 1, :]
    cs = cos[None, :, None].astype(x.dtype)
    sn = sin[None, :, None].astype(x.dtype)
    out = jnp.stack([x1 * cs - x2 * sn, x2 * cs + x1 * sn], axis=-2)
    return out.reshape(x.shape)


def differential_attention(q, k, v, lam_params, subln_g, lambda_init):
    bsz, n_q = q.shape[:2]
    n_k = k.shape[1]
    lp = lam_params.astype(jnp.float32)
    lam = jnp.exp(jnp.sum(lp[0] * lp[1])) - jnp.exp(jnp.sum(lp[2] * lp[3])) + lambda_init
    q = q * (ATT_HEAD_DIM ** -0.5)
    n_blocks = n_q // Q_BLOCK
    qb = q.reshape(bsz, n_blocks, Q_BLOCK, 2 * ATT_HEADS, ATT_HEAD_DIM).transpose(1, 0, 2, 3, 4)

    def one_block(qi):
        s = jnp.einsum('bqhd,bkhd->bhqk', qi, k).astype(jnp.float32)
        p = jax.nn.softmax(s, axis=-1).reshape(bsz, ATT_HEADS, 2, Q_BLOCK, n_k)
        a = p[:, :, 0] - lam * p[:, :, 1]
        return jnp.einsum('bhqk,bkhe->bqhe', a.astype(v.dtype), v)

    o = lax.map(one_block, qb)
    o = o.transpose(1, 0, 2, 3, 4).reshape(bsz, n_q, ATT_HEADS, 2 * ATT_HEAD_DIM)
    return rms_norm(o, subln_g) * (1.0 - lambda_init)


def zoh_discretize(a_re, a_im, log_dt, b_re, b_im):
    dt = jnp.exp(log_dt)[:, None]
    mag = jnp.exp(dt * a_re)
    ang = dt * a_im
    ab_re, ab_im = mag * jnp.cos(ang), mag * jnp.sin(ang)
    den = a_re * a_re + a_im * a_im
    xm = ab_re - 1.0
    f_re = (xm * a_re + ab_im * a_im) / den
    f_im = (ab_im * a_re - xm * a_im) / den
    bb_re = f_re[..., None] * b_re - f_im[..., None] * b_im
    bb_im = f_re[..., None] * b_im + f_im[..., None] * b_re
    return ab_re, ab_im, bb_re, bb_im


def complex_affine_combine(left, right):
    a1r, a1i, b1r, b1i = left
    a2r, a2i, b2r, b2i = right
    return (a2r * a1r - a2i * a1i, a2r * a1i + a2i * a1r,
            a2r * b1r - a2i * b1i + b2r, a2r * b1i + a2i * b1r + b2i)


def bidirectional_s5(u, a_re, a_im, log_dt, b, c, d, h0):
    bsz, n_tok, _ = u.shape
    uf = u.astype(jnp.float32).reshape(bsz, n_tok, SSM_GROUPS, SSM_GROUP_CH)
    y = uf * d.astype(jnp.float32)
    finals = []
    for direction in range(2):
        reverse = direction == 1
        ab_re, ab_im, bb_re, bb_im = zoh_discretize(
            a_re[direction].astype(jnp.float32), a_im[direction].astype(jnp.float32),
            log_dt[direction].astype(jnp.float32),
            b[direction, 0].astype(jnp.float32), b[direction, 1].astype(jnp.float32))
        bu_re = jnp.einsum('blgp,gnp->blgn', uf, bb_re)
        bu_im = jnp.einsum('blgp,gnp->blgn', uf, bb_im)
        if h0 is not None:
            s_re = h0[:, direction, 0].astype(jnp.float32)
            s_im = h0[:, direction, 1].astype(jnp.float32)
            first = n_tok - 1 if reverse else 0
            bu_re = bu_re.at[:, first].add(ab_re * s_re - ab_im * s_im)
            bu_im = bu_im.at[:, first].add(ab_re * s_im + ab_im * s_re)
        shape_a = (1, n_tok, SSM_GROUPS, SSM_STATE)
        _, _, x_re, x_im = lax.associative_scan(
            complex_affine_combine,
            (jnp.broadcast_to(ab_re, shape_a), jnp.broadcast_to(ab_im, shape_a), bu_re, bu_im),
            reverse=reverse, axis=1)
        c_re = c[direction, 0].astype(jnp.float32)
        c_im = c[direction, 1].astype(jnp.float32)
        y = y + jnp.einsum('blgn,gpn->blgp', x_re, c_re) - jnp.einsum('blgn,gpn->blgp', x_im, c_im)
        if h0 is None:
            last = 0 if reverse else n_tok - 1
            finals.append(jnp.stack([x_re[:, last], x_im[:, last]], axis=1))
    final_state = jnp.stack(finals, axis=1).astype(u.dtype) if h0 is None else None
    return y.reshape(bsz, n_tok, SSM_WIDTH).astype(u.dtype), final_state


def conformer_conv(zc, conv_w, conv_b, ln_g, ln_b, w_out, b_out):
    x = zc[..., :CONV_CH] * jax.nn.sigmoid(zc[..., CONV_CH:])
    x = lax.conv_general_dilated(x, conv_w[:, None, :], window_strides=(1,),
                                 padding=[(CONV_PAD, CONV_PAD)],
                                 dimension_numbers=('NWC', 'WIO', 'NWC'),
                                 feature_group_count=CONV_CH) + conv_b
    x = jax.nn.silu(layer_norm(x, ln_g, ln_b))
    return x @ w_out + b_out


def moe_ffn(h, router_w, router_b, w_in, b_in, w_out, b_out):
    n_tok = h.shape[0]
    n_assign = n_tok * TOP_K
    logits = (h @ router_w + router_b).astype(jnp.float32)
    top_logit, top_idx = lax.top_k(logits, TOP_K)
    gate = jax.nn.softmax(top_logit, axis=-1).reshape(-1)
    flat_e = top_idx.reshape(-1)
    order = jnp.argsort(flat_e)
    sorted_e = flat_e[order]
    counts = jnp.bincount(flat_e, length=N_EXPERTS)
    padded = (counts + MOE_BLOCK - 1) // MOE_BLOCK * MOE_BLOCK
    start = jnp.cumsum(counts) - counts
    pad_end = jnp.cumsum(padded)
    pad_start = pad_end - padded
    dest = pad_start[sorted_e] + jnp.arange(n_assign) - start[sorted_e]
    n_blocks = -(-n_assign // MOE_BLOCK) + N_EXPERTS
    n_rows = n_blocks * MOE_BLOCK
    row_tok = jnp.zeros((n_rows,), jnp.int32).at[dest].set((order // TOP_K).astype(jnp.int32))
    row_gate = jnp.zeros((n_rows,), jnp.float32).at[dest].set(gate[order])
    block_e = jnp.minimum(jnp.searchsorted(pad_end, jnp.arange(n_blocks) * MOE_BLOCK, side='right'),
                          N_EXPERTS - 1)
    xs = h[row_tok].reshape(n_blocks, MOE_BLOCK, h.shape[1])

    def expert_block(args):
        xb, e = args
        zz = xb @ w_in[e] + b_in[e]
        g, up = zz[:, :EXPERT_FF], zz[:, EXPERT_FF:]
        g = jnp.minimum(g, SWIGLU_LIMIT)
        up = jnp.clip(up, -SWIGLU_LIMIT, SWIGLU_LIMIT)
        return ((up + 1.0) * (g * jax.nn.sigmoid(SWIGLU_ALPHA * g))) @ w_out[e] + b_out[e]

    ys = lax.map(expert_block, (xs, block_e)).reshape(n_rows, h.shape[1])
    return jnp.zeros_like(h).at[row_tok].add(ys * row_gate[:, None].astype(h.dtype))


def token_mixer(h, lp, lambda_init, rope, ctx):
    bsz, n_tok, _ = h.shape
    z = h @ lp['w_in']
    q, k, v, u, zc = jnp.split(z, IN_SPLITS, axis=-1)
    q = q.reshape(bsz, n_tok, 2 * ATT_HEADS, ATT_HEAD_DIM)
    k = k.reshape(bsz, n_tok, 2 * ATT_HEADS, ATT_HEAD_DIM)
    v = v.reshape(bsz, n_tok, ATT_HEADS, 2 * ATT_HEAD_DIM)
    if ctx is None:
        q_att, k_att, v_att, h0 = q, k, v, None
    else:
        ctx_k, ctx_v, h0 = ctx
        cos, sin = rope
        q_att = apply_rope(q, cos, sin)
        k_att = jnp.concatenate([apply_rope(k, cos, sin), ctx_k], axis=1)
        v_att = jnp.concatenate([v, ctx_v], axis=1)
    attn = differential_attention(q_att, k_att, v_att, lp['diff_lambda'], lp['diff_subln'], lambda_init)
    attn_out = attn.reshape(bsz, n_tok, ATT_V_WIDTH) @ lp['w_attn_out']

    ssm_y, ssm_final = bidirectional_s5(u, lp['ssm_a_re'], lp['ssm_a_im'], lp['ssm_log_dt'],
                                        lp['ssm_b'], lp['ssm_c'], lp['ssm_d'], h0)
    glu_a, glu_b = jnp.split(jax.nn.gelu(ssm_y) @ lp['w_ssm_glu'], 2, axis=-1)
    ssm_out = glu_a * jax.nn.sigmoid(glu_b)

    conv_out = conformer_conv(zc, lp['conv_w'], lp['conv_b'], lp['conv_ln_g'], lp['conv_ln_b'],
                              lp['w_conv_out'], lp['b_conv_out'])

    gates = jax.nn.sigmoid(h @ lp['w_merge_gate'] + lp['b_merge_gate'])
    g_att, g_ssm, g_conv = jnp.split(gates, N_BRANCHES, axis=-1)
    out = (g_att * attn_out + g_ssm * ssm_out + g_conv * conv_out) @ lp['w_out']
    ctx_out = (k, v, ssm_final) if ctx is None else None
    return out, ctx_out


def trunk_layer(x, cond, lp, lambda_init, rope, ctx):
    mod = jax.nn.silu(cond) @ lp['w_ada'] + lp['b_ada']
    sh1, sc1, g1, sh2, sc2, g2 = jnp.split(mod, 6, axis=-1)
    h = rms_norm(x, lp['norm_mix']) * (1.0 + sc1) + sh1
    mix, ctx_out = token_mixer(h, lp, lambda_init, rope, ctx)
    x = x + g1 * mix
    h = rms_norm(x, lp['norm_ffn']) * (1.0 + sc2) + sh2
    ffn = moe_ffn(h.reshape(-1, D_MODEL), lp['router_w'], lp['router_b'], lp['moe_w_in'],
                  lp['moe_b_in'], lp['moe_w_out'], lp['moe_b_out']).reshape(x.shape)
    return x + g2 * ffn, ctx_out


def setup_inputs(seed: int = 0) -> dict:
    key = jax.random.key(seed)
    ks = iter(jax.random.split(key, 40))
    f32 = jnp.float32

    def nrm(shape, scale):
        return jax.random.normal(next(ks), shape, f32) * scale

    D, G, N, P = D_MODEL, SSM_GROUPS, SSM_STATE, SSM_GROUP_CH
    x_prompt = nrm((BATCH, SEQ, D), 1.0)
    x_sample = nrm((DEC_BATCH, DEC_SEQ, D), 1.0)
    cache_k = nrm((DEC_BATCH, DEPTH, PAST_LEN, 2 * ATT_HEADS, ATT_HEAD_DIM), 1.0)
    cache_v = nrm((DEC_BATCH, DEPTH, PAST_LEN, ATT_HEADS, 2 * ATT_HEAD_DIM), 1.0)
    state_ssm = nrm((DEC_BATCH, DEPTH, 2, 2, G, N), 0.5)
    c = nrm((DEC_BATCH, D), 1.0)
    c_ctx = nrm((D,), 1.0)
    w_ada = nrm((DEPTH, D, 6 * D), 0.5 * D ** -0.5)
    b_ada = nrm((DEPTH, 6 * D), 0.01)
    norm_mix = 1.0 + nrm((DEPTH, D), 0.05)
    norm_ffn = 1.0 + nrm((DEPTH, D), 0.05)
    w_in = nrm((DEPTH, D, IN_WIDTH), D ** -0.5)
    diff_lambda = nrm((DEPTH, 4, ATT_HEAD_DIM), 0.1)
    diff_subln = 1.0 + nrm((DEPTH, 2 * ATT_HEAD_DIM), 0.05)
    w_attn_out = nrm((DEPTH, ATT_V_WIDTH, D), ATT_V_WIDTH ** -0.5)
    ssm_a_re = -0.5 * jnp.exp(nrm((DEPTH, 2, G, N), 0.05))
    ssm_a_im = jnp.pi * jnp.arange(N, dtype=f32) + nrm((DEPTH, 2, G, N), 0.01)
    ssm_log_dt = jax.random.uniform(next(ks), (DEPTH, 2, G), f32, math.log(1e-3), math.log(1e-1))
    ssm_b = nrm((DEPTH, 2, 2, G, N, P), P ** -0.5)
    ssm_c = nrm((DEPTH, 2, 2, G, P, N), N ** -0.5)
    ssm_d = nrm((DEPTH, G, P), 1.0)
    w_ssm_glu = nrm((DEPTH, SSM_WIDTH, 2 * D), SSM_WIDTH ** -0.5)
    conv_w = nrm((DEPTH, CONV_K, CONV_CH), CONV_K ** -0.5)
    conv_b = nrm((DEPTH, CONV_CH), 0.01)
    conv_ln_g = 1.0 + nrm((DEPTH, CONV_CH), 0.05)
    conv_ln_b = nrm((DEPTH, CONV_CH), 0.01)
    w_conv_out = nrm((DEPTH, CONV_CH, D), CONV_CH ** -0.5)
    b_conv_out = nrm((DEPTH, D), 0.01)
    w_merge_gate = nrm((DEPTH, D, N_BRANCHES * D), D ** -0.5)
    b_merge_gate = nrm((DEPTH, N_BRANCHES * D), 0.01)
    w_out = nrm((DEPTH, D, D), D ** -0.5)
    router_w = nrm((DEPTH, D, N_EXPERTS), D ** -0.5)
    router_b = nrm((DEPTH, N_EXPERTS), 0.01)
    moe_w_in = nrm((DEPTH, N_EXPERTS, D, 2 * EXPERT_FF), D ** -0.5)
    moe_b_in = nrm((DEPTH, N_EXPERTS, 2 * EXPERT_FF), 0.01)
    moe_w_out = nrm((DEPTH, N_EXPERTS, EXPERT_FF, D), EXPERT_FF ** -0.5)
    moe_b_out = nrm((DEPTH, N_EXPERTS, D), 0.01)
    norm_final = 1.0 + nrm((D,), 0.05)
    return {'x_prompt': x_prompt, 'x_sample': x_sample, 'cache_k': cache_k, 'cache_v': cache_v,
            'state_ssm': state_ssm, 'c': c, 'c_ctx': c_ctx, 'w_ada': w_ada, 'b_ada': b_ada,
            'norm_mix': norm_mix, 'norm_ffn': norm_ffn, 'w_in': w_in, 'diff_lambda': diff_lambda,
            'diff_subln': diff_subln, 'w_attn_out': w_attn_out, 'ssm_a_re': ssm_a_re,
            'ssm_a_im': ssm_a_im, 'ssm_log_dt': ssm_log_dt, 'ssm_b': ssm_b, 'ssm_c': ssm_c,
            'ssm_d': ssm_d, 'w_ssm_glu': w_ssm_glu, 'conv_w': conv_w, 'conv_b': conv_b,
            'conv_ln_g': conv_ln_g, 'conv_ln_b': conv_ln_b, 'w_conv_out': w_conv_out,
            'b_conv_out': b_conv_out, 'w_merge_gate': w_merge_gate, 'b_merge_gate': b_merge_gate,
            'w_out': w_out, 'router_w': router_w, 'router_b': router_b, 'moe_w_in': moe_w_in,
            'moe_b_in': moe_b_in, 'moe_w_out': moe_w_out, 'moe_b_out': moe_b_out,
            'norm_final': norm_final}


def reference(x_prompt, x_sample, cache_k, cache_v, state_ssm, c, c_ctx, w_ada, b_ada, norm_mix,
              norm_ffn, w_in, diff_lambda, diff_subln, w_attn_out, ssm_a_re, ssm_a_im, ssm_log_dt,
              ssm_b, ssm_c, ssm_d, w_ssm_glu, conv_w, conv_b, conv_ln_g, conv_ln_b, w_conv_out,
              b_conv_out, w_merge_gate, b_merge_gate, w_out, router_w, router_b, moe_w_in,
              moe_b_in, moe_w_out, moe_b_out, norm_final):
    rope = axial_rope_tables(x_sample.shape[1])
    cond_ctx = c_ctx[None, None, :]
    cond_lat = c[:, None, :]
    xc, xl = x_prompt, x_sample
    new_k, new_v, new_s = [], [], []
    for l in range(DEPTH):
        lp = dict(w_ada=w_ada[l], b_ada=b_ada[l], norm_mix=norm_mix[l], norm_ffn=norm_ffn[l],
                  w_in=w_in[l], diff_lambda=diff_lambda[l], diff_subln=diff_subln[l],
                  w_attn_out=w_attn_out[l], ssm_a_re=ssm_a_re[l], ssm_a_im=ssm_a_im[l],
                  ssm_log_dt=ssm_log_dt[l], ssm_b=ssm_b[l], ssm_c=ssm_c[l], ssm_d=ssm_d[l],
                  w_ssm_glu=w_ssm_glu[l], conv_w=conv_w[l], conv_b=conv_b[l],
                  conv_ln_g=conv_ln_g[l], conv_ln_b=conv_ln_b[l], w_conv_out=w_conv_out[l],
                  b_conv_out=b_conv_out[l], w_merge_gate=w_merge_gate[l],
                  b_merge_gate=b_merge_gate[l], w_out=w_out[l], router_w=router_w[l],
                  router_b=router_b[l], moe_w_in=moe_w_in[l], moe_b_in=moe_b_in[l],
                  moe_w_out=moe_w_out[l], moe_b_out=moe_b_out[l])
        lambda_init = 0.8 - 0.6 * math.exp(-0.3 * l)
        xc, (k_l, v_l, s_l) = trunk_layer(xc, cond_ctx, lp, lambda_init, None, None)
        new_k.append(k_l)
        new_v.append(v_l)
        new_s.append(s_l)
        xl, _ = trunk_layer(xl, cond_lat, lp, lambda_init, rope,
                            (cache_k[:, l], cache_v[:, l], state_ssm[:, l]))
    y_prompt = rms_norm(xc, norm_final)
    y_sample = rms_norm(xl, norm_final)
    new_cache_k = jnp.stack(new_k, axis=1)
    new_cache_v = jnp.stack(new_v, axis=1)
    new_state_ssm = jnp.stack(new_s, axis=1)
    return (y_prompt, y_sample, new_cache_k, new_cache_v, new_state_ssm)
```

```python
import functools
import math

import jax
import jax.numpy as jnp
from jax import lax
from jax.experimental import pallas as pl
from jax.experimental.pallas import tpu as pltpu

F32 = jnp.float32
BF16 = jnp.bfloat16

D = 2048
BATCH = 32
SEQ = 256
DEPTH = 4
DEC_BATCH = 2
DEC_SEQ = 2048
PAST = 512
GRID_W = 64
HEADS = 8
HEAD_DIM = 64
ROPE_PAIRS = 16
ROPE_BASE = 10000.0
QKV_W = 1024
SSM_W = 1024
SSM_G = 64
SSM_P = 16
SSM_N = 64
CONV_CH = 1024
CONV_K = 31
IN_W = 6144
N_EXP = 32
TOP_K = 4
FF = 2048
LIMIT = 7.0
ALPHA = 1.702
EPS = 1e-6

T_CTX = BATCH * SEQ
T_LAT = DEC_BATCH * DEC_SEQ
T = T_CTX + T_LAT
BLK = 256
N_BLK = T // BLK
CTX_BLKS = T_CTX // BLK
LAT_BLKS = DEC_SEQ // BLK
N_GRP = 1 + DEC_BATCH
SUB = 8
LANE = 128
STATE_CH = 1024
U_CH = 256
N_CC = SSM_W // U_CH
N_SG = N_BLK // SUB

MOE_BM = 1024
MOE_SUB = 256
MOE_TF = 256
MOE_NF = FF // MOE_TF
MOE_BLOCKS = (T * TOP_K) // MOE_BM + N_EXP
MOE_ROWS = MOE_BLOCKS * MOE_BM

VMEM_LIMIT = 56 << 20


def _cp(*sem, vmem=VMEM_LIMIT):
    return pltpu.CompilerParams(dimension_semantics=sem, vmem_limit_bytes=vmem)


def _grp_of_row(row):
    return jnp.where(row < T_CTX, 0, 1 + (row - T_CTX) // DEC_SEQ)


def _ada_kernel(c_ref, w_ref, b_ref, o_ref):
    c = c_ref[...]
    a = (c * jax.nn.sigmoid(c)).astype(BF16)
    o_ref[...] = jnp.dot(a, w_ref[...].astype(BF16), preferred_element_type=F32) + b_ref[...]


def ada_all(cond8, w_ada, b_ada):
    tn = 1024
    n = 6 * D
    return pl.pallas_call(
        _ada_kernel,
        grid=(DEPTH, n // tn),
        in_specs=[pl.BlockSpec((SUB, D), lambda l, j: (0, 0)),
                  pl.BlockSpec((None, D, tn), lambda l, j: (l, 0, j)),
                  pl.BlockSpec((None, 1, tn), lambda l, j: (l, 0, j))],
        out_specs=pl.BlockSpec((None, SUB, tn), lambda l, j: (l, 0, j)),
        out_shape=jax.ShapeDtypeStruct((DEPTH, SUB, n), F32),
        compiler_params=_cp("parallel", "parallel"),
        name="ada",
    )(cond8, w_ada, b_ada.reshape(DEPTH, 1, n))


def _rms(x, g):
    return x * lax.rsqrt(jnp.mean(x * x, axis=-1, keepdims=True) + EPS) * g


def _norm_mod_kernel(x_ref, g_ref, mod_ref, o_ref):
    y = _rms(x_ref[...], g_ref[...])
    sh = mod_ref[0:1, :]
    sc = mod_ref[1:2, :]
    o_ref[...] = (y * (1.0 + sc) + sh).astype(o_ref.dtype)


def norm_mod(x, g, mod, l):
    tm = 512
    return pl.pallas_call(
        _norm_mod_kernel,
        grid=(T // tm,),
        in_specs=[pl.BlockSpec((tm, D), lambda m: (m, 0)),
                  pl.BlockSpec((None, 1, D), lambda m: (l, 0, 0)),
                  pl.BlockSpec((None, None, 6, D), lambda m: (l, _grp_of_row(m * tm), 0, 0))],
        out_specs=pl.BlockSpec((tm, D), lambda m: (m, 0)),
        out_shape=jax.ShapeDtypeStruct((T, D), BF16),
        compiler_params=_cp("parallel"),
        name="norm_mod",
    )(x, g.reshape(DEPTH, 1, D), mod)


def _final_norm_kernel(x_ref, g_ref, o_ref):
    o_ref[...] = _rms(x_ref[...], g_ref[...])


def final_norm(x, g):
    tm = 512
    return pl.pallas_call(
        _final_norm_kernel,
        grid=(T // tm,),
        in_specs=[pl.BlockSpec((tm, D), lambda m: (m, 0)),
                  pl.BlockSpec((1, D), lambda m: (0, 0))],
        out_specs=pl.BlockSpec((tm, D), lambda m: (m, 0)),
        out_shape=jax.ShapeDtypeStruct((T, D), F32),
        compiler_params=_cp("parallel"),
        name="final_norm",
    )(x, g.reshape(1, D))


def _mm_kernel(a_ref, w_ref, b_ref, o_ref, wb_ref):
    @pl.when(pl.program_id(1) == 0)
    def _():
        wb_ref[...] = w_ref[...].astype(BF16)

    acc = jnp.dot(a_ref[...], wb_ref[...], preferred_element_type=F32)
    o_ref[...] = (acc + b_ref[...]).astype(o_ref.dtype)


def matmul(a, w, bias, l, *, tm=1024, tn=1024, out_dtype=F32, name="mm"):
    k, n = w.shape[1], w.shape[2]
    return pl.pallas_call(
        _mm_kernel,
        grid=(n // tn, T // tm),
        in_specs=[pl.BlockSpec((tm, k), lambda j, m: (m, 0)),
                  pl.BlockSpec((None, k, tn), lambda j, m: (l, 0, j)),
                  pl.BlockSpec((None, 1, tn), lambda j, m: (l, 0, j))],
        out_specs=pl.BlockSpec((tm, tn), lambda j, m: (m, j)),
        out_shape=jax.ShapeDtypeStruct((T, n), out_dtype),
        scratch_shapes=[pltpu.VMEM((k, tn), BF16)],
        compiler_params=_cp("parallel", "arbitrary"),
        name=name,
    )(a, w, bias)


def _mm_glu_kernel(a_ref, wa_ref, wg_ref, o_ref, wab_ref, wgb_ref):
    @pl.when(pl.program_id(1) == 0)
    def _():
        wab_ref[...] = wa_ref[...].astype(BF16)
        wgb_ref[...] = wg_ref[...].astype(BF16)

    a = a_ref[...]
    va = jnp.dot(a, wab_ref[...], preferred_element_type=F32)
    vg = jnp.dot(a, wgb_ref[...], preferred_element_type=F32)
    o_ref[...] = va * jax.nn.sigmoid(vg)


def matmul_glu(a, w, l, *, tm=1024, tn=512):
    k, n = w.shape[1], w.shape[2] // 2
    nj = n // tn
    return pl.pallas_call(
        _mm_glu_kernel,
        grid=(nj, T // tm),
        in_specs=[pl.BlockSpec((tm, k), lambda j, m: (m, 0)),
                  pl.BlockSpec((None, k, tn), lambda j, m: (l, 0, j)),
                  pl.BlockSpec((None, k, tn), lambda j, m: (l, 0, nj + j))],
        out_specs=pl.BlockSpec((tm, tn), lambda j, m: (m, j)),
        out_shape=jax.ShapeDtypeStruct((T, n), F32),
        scratch_shapes=[pltpu.VMEM((k, tn), BF16), pltpu.VMEM((k, tn), BF16)],
        compiler_params=_cp("parallel", "arbitrary"),
        name="ssm_glu",
    )(a, w, w)


def _mm_resid_kernel(a_ref, w_ref, x_ref, mod_ref, o_ref, wb_ref, *, gate_row):
    @pl.when(pl.program_id(1) == 0)
    def _():
        wb_ref[...] = w_ref[...].astype(BF16)

    acc = jnp.dot(a_ref[...], wb_ref[...], preferred_element_type=F32)
    o_ref[...] = x_ref[...] + mod_ref[gate_row:gate_row + 1, :] * acc


def matmul_resid(a, w, x, mod, l, *, gate_row, tm=1024, tn=1024):
    k, n = w.shape[1], w.shape[2]
    return pl.pallas_call(
        functools.partial(_mm_resid_kernel, gate_row=gate_row),
        grid=(n // tn, T // tm),
        in_specs=[pl.BlockSpec((tm, k), lambda j, m: (m, 0)),
                  pl.BlockSpec((None, k, tn), lambda j, m: (l, 0, j)),
                  pl.BlockSpec((tm, tn), lambda j, m: (m, j)),
                  pl.BlockSpec((None, None, 6, tn), lambda j, m: (l, _grp_of_row(m * tm), 0, j))],
        out_specs=pl.BlockSpec((tm, tn), lambda j, m: (m, j)),
        out_shape=jax.ShapeDtypeStruct((T, n), F32),
        scratch_shapes=[pltpu.VMEM((k, tn), BF16)],
        compiler_params=_cp("parallel", "arbitrary"),
        name="out_resid",
    )(a, w, x, mod)


def _merge_kernel(h_ref, w0_ref, w1_ref, w2_ref, b0_ref, b1_ref, b2_ref,
                  att_ref, ssm_ref, cnv_ref, o_ref, wb0_ref, wb1_ref, wb2_ref):
    @pl.when(pl.program_id(1) == 0)
    def _():
        wb0_ref[...] = w0_ref[...].astype(BF16)
        wb1_ref[...] = w1_ref[...].astype(BF16)
        wb2_ref[...] = w2_ref[...].astype(BF16)

    h = h_ref[...]

    def gate(wb_ref, b_ref):
        return jax.nn.sigmoid(jnp.dot(h, wb_ref[...], preferred_element_type=F32) + b_ref[...])

    mixed = (gate(wb0_ref, b0_ref) * att_ref[...] + gate(wb1_ref, b1_ref) * ssm_ref[...]
             + gate(wb2_ref, b2_ref) * cnv_ref[...])
    o_ref[...] = mixed.astype(o_ref.dtype)


def merge_gates(h, w, b, att, ssm, cnv, l, *, tm=512, tn=512):
    nj = D // tn
    wspec = lambda br: pl.BlockSpec((None, D, tn), lambda j, m: (l, 0, br * nj + j))
    bspec = lambda br: pl.BlockSpec((None, 1, tn), lambda j, m: (l, 0, br * nj + j))
    tile = pl.BlockSpec((tm, tn), lambda j, m: (m, j))
    return pl.pallas_call(
        _merge_kernel,
        grid=(nj, T // tm),
        in_specs=[pl.BlockSpec((tm, D), lambda j, m: (m, 0)),
                  wspec(0), wspec(1), wspec(2), bspec(0), bspec(1), bspec(2),
                  tile, tile, tile],
        out_specs=tile,
        out_shape=jax.ShapeDtypeStruct((T, D), BF16),
        scratch_shapes=[pltpu.VMEM((D, tn), BF16)] * 3,
        compiler_params=_cp("parallel", "arbitrary"),
        name="merge_gates",
    )(h, w, w, w, b, b, b, att, ssm, cnv)


def _lambda(lam_ref, lambda_init):
    lp = lam_ref[...]
    s01 = jnp.sum(lp[0:1, :] * lp[1:2, :], axis=-1, keepdims=True)
    s23 = jnp.sum(lp[2:3, :] * lp[3:4, :], axis=-1, keepdims=True)
    return jnp.exp(s01) - jnp.exp(s23) + lambda_init


def _softmax(s):
    e = jnp.exp(s - jnp.max(s, axis=-1, keepdims=True))
    return e * (1.0 / jnp.sum(e, axis=-1, keepdims=True))


def _diff_attn_head(q, kb, vb, lam, g, lambda_init):
    lane = lax.broadcasted_iota(jnp.int32, q.shape, 1)
    q_a = jnp.where(lane < HEAD_DIM, q, 0.0).astype(BF16)
    q_b = jnp.where(lane >= HEAD_DIM, q, 0.0).astype(BF16)
    nt = (((1,), (1,)), ((), ()))
    p_a = _softmax(lax.dot_general(q_a, kb, nt, preferred_element_type=F32))
    p_b = _softmax(lax.dot_general(q_b, kb, nt, preferred_element_type=F32))
    a = (p_a - lam * p_b).astype(BF16)
    o = jnp.dot(a, vb, preferred_element_type=F32)
    return _rms(o, g) * (1.0 - lambda_init)


def _attn_ctx_kernel(q_ref, k_ref, v_ref, lam_ref, g_ref, o_ref, *, lambda_init):
    lam = _lambda(lam_ref, lambda_init)
    g = g_ref[...]
    for h in range(HEADS):
        cols = slice(h * LANE, (h + 1) * LANE)
        q = q_ref[:, cols] * (HEAD_DIM ** -0.5)
        o = _diff_attn_head(q, k_ref[:, cols].astype(BF16), v_ref[:, cols].astype(BF16),
                            lam, g, lambda_init)
        o_ref[:, cols] = o.astype(o_ref.dtype)


def attn_ctx(z, lam_p, subln, l, lambda_init):
    return pl.pallas_call(
        functools.partial(_attn_ctx_kernel, lambda_init=lambda_init),
        grid=(BATCH,),
        in_specs=[pl.BlockSpec((SEQ, QKV_W), lambda b: (b, 0)),
                  pl.BlockSpec((SEQ, QKV_W), lambda b: (b, 1)),
                  pl.BlockSpec((SEQ, QKV_W), lambda b: (b, 2)),
                  pl.BlockSpec((None, 4, HEAD_DIM), lambda b: (l, 0, 0)),
                  pl.BlockSpec((None, 1, LANE), lambda b: (l, 0, 0))],
        out_specs=pl.BlockSpec((SEQ, QKV_W), lambda b: (b, 0)),
        out_shape=jax.ShapeDtypeStruct((T, QKV_W), BF16),
        compiler_params=_cp("parallel"),
        name="attn_ctx",
    )(z, z, z, lam_p, subln.reshape(DEPTH, 1, LANE))


def _rope(x, cos, sin_signed):
    lane = lax.broadcasted_iota(jnp.int32, x.shape, 1)
    first = (lane % (2 * ROPE_PAIRS)) < ROPE_PAIRS
    partner = jnp.where(first, pltpu.roll(x, LANE - ROPE_PAIRS, 1), pltpu.roll(x, ROPE_PAIRS, 1))
    return x * cos + partner * sin_signed


def _attn_lat_kernel(q_ref, k_ref, v_ref, ck_ref, cv_ref, cq_ref, sq_ref, ckk_ref, skk_ref,
                     lam_ref, g_ref, prev_ref, o_ref, kb_ref, vb_ref, *, lambda_init):
    del prev_ref

    @pl.when(pl.program_id(2) == 0)
    def _():
        kb_ref[0:DEC_SEQ, :] = _rope(k_ref[...], ckk_ref[...], skk_ref[...]).astype(BF16)
        kb_ref[DEC_SEQ:DEC_SEQ + PAST, :] = ck_ref[...].astype(BF16)
        vb_ref[0:DEC_SEQ, :] = v_ref[...].astype(BF16)
        vb_ref[DEC_SEQ:DEC_SEQ + PAST, :] = cv_ref[...].astype(BF16)

    q = _rope(q_ref[...], cq_ref[...], sq_ref[...]) * (HEAD_DIM ** -0.5)
    o = _diff_attn_head(q, kb_ref[...], vb_ref[...], _lambda(lam_ref, lambda_init), g_ref[...],
                        lambda_init)
    o_ref[...] = o.astype(o_ref.dtype)


def attn_lat(z, cache_k, cache_v, cos_t, sin_t, lam_p, subln, prev, l, lambda_init):
    tq = BLK
    nq = DEC_SEQ // tq
    qrow = lambda b, h, i: CTX_BLKS + b * nq + i
    krow = lambda b: T_CTX // DEC_SEQ + b
    return pl.pallas_call(
        functools.partial(_attn_lat_kernel, lambda_init=lambda_init),
        grid=(DEC_BATCH, HEADS, nq),
        in_specs=[pl.BlockSpec((tq, LANE), lambda b, h, i: (qrow(b, h, i), h)),
                  pl.BlockSpec((DEC_SEQ, LANE), lambda b, h, i: (krow(b), HEADS + h)),
                  pl.BlockSpec((DEC_SEQ, LANE), lambda b, h, i: (krow(b), 2 * HEADS + h)),
                  pl.BlockSpec((None, None, PAST, LANE), lambda b, h, i: (b, l, 0, h)),
                  pl.BlockSpec((None, None, PAST, LANE), lambda b, h, i: (b, l, 0, h)),
                  pl.BlockSpec((tq, LANE), lambda b, h, i: (i, 0)),
                  pl.BlockSpec((tq, LANE), lambda b, h, i: (i, 0)),
                  pl.BlockSpec((DEC_SEQ, LANE), lambda b, h, i: (0, 0)),
                  pl.BlockSpec((DEC_SEQ, LANE), lambda b, h, i: (0, 0)),
                  pl.BlockSpec((None, 4, HEAD_DIM), lambda b, h, i: (l, 0, 0)),
                  pl.BlockSpec((None, 1, LANE), lambda b, h, i: (l, 0, 0)),
                  pl.BlockSpec(memory_space=pl.ANY)],
        out_specs=pl.BlockSpec((tq, LANE), lambda b, h, i: (qrow(b, h, i), h)),
        out_shape=jax.ShapeDtypeStruct((T, QKV_W), BF16),
        scratch_shapes=[pltpu.VMEM((DEC_SEQ + PAST, LANE), BF16)] * 2,
        input_output_aliases={11: 0},
        compiler_params=_cp("parallel", "parallel", "arbitrary"),
        name="attn_lat",
    )(z, z, z, cache_k, cache_v, cos_t, sin_t, cos_t, sin_t, lam_p,
      subln.reshape(DEPTH, 1, LANE), prev)


def rope_tables():
    row = (jnp.arange(DEC_SEQ) // GRID_W).astype(F32)
    col = (jnp.arange(DEC_SEQ) % GRID_W).astype(F32)
    inv_freq = ROPE_BASE ** (-jnp.arange(ROPE_PAIRS, dtype=F32) / ROPE_PAIRS)
    ang = jnp.stack([row[:, None] * inv_freq, col[:, None] * inv_freq], axis=1)
    cos, sin = jnp.cos(ang), jnp.sin(ang)
    cos64 = jnp.concatenate([cos, cos], axis=-1).reshape(DEC_SEQ, HEAD_DIM)
    sin64 = jnp.concatenate([-sin, sin], axis=-1).reshape(DEC_SEQ, HEAD_DIM)
    return jnp.tile(cos64, (1, 2)), jnp.tile(sin64, (1, 2))


def _zoh_kernel(are_ref, aim_ref, ldt_ref, bre_ref, bim_ref,
                abr_ref, abi_ref, bbr_ref, bbi_ref):
    a_re = are_ref[...]
    a_im = aim_ref[...]
    dt = jnp.exp(ldt_ref[...])
    mag = jnp.exp(dt * a_re)
    ang = dt * a_im
    ab_re = mag * jnp.cos(ang)
    ab_im = mag * jnp.sin(ang)
    den = a_re * a_re + a_im * a_im
    xm = ab_re - 1.0
    f_re = (xm * a_re + ab_im * a_im) / den
    f_im = (ab_im * a_re - xm * a_im) / den
    abr_ref[...] = ab_re
    abi_ref[...] = ab_im
    rows = a_re.shape[0]
    fr = jnp.broadcast_to(f_re[:, None, :], (rows, SSM_P, SSM_N))
    fi = jnp.broadcast_to(f_im[:, None, :], (rows, SSM_P, SSM_N))
    b_re = bre_ref[...]
    b_im = bim_ref[...]
    bbr_ref[...] = fr * b_re - fi * b_im
    bbi_ref[...] = fr * b_im + fi * b_re


def ssm_discretize(a_re, a_im, log_dt, b):
    rows = DEPTH * 2 * SSM_G
    bt = jnp.swapaxes(b, -1, -2)
    b_re = bt[:, :, 0].reshape(rows, SSM_P, SSM_N)
    b_im = bt[:, :, 1].reshape(rows, SSM_P, SSM_N)
    sd = jax.ShapeDtypeStruct
    ab_re, ab_im, bb_re, bb_im = pl.pallas_call(
        _zoh_kernel,
        out_shape=[sd((rows, SSM_N), F32), sd((rows, SSM_N), F32),
                   sd((rows, SSM_P, SSM_N), F32), sd((rows, SSM_P, SSM_N), F32)],
        compiler_params=pltpu.CompilerParams(vmem_limit_bytes=VMEM_LIMIT),
        name="ssm_zoh",
    )(a_re.reshape(rows, SSM_N), a_im.reshape(rows, SSM_N), log_dt.reshape(rows, 1), b_re, b_im)
    return ab_re, ab_im, bb_re, bb_im


def _block_diag(m):
    eye = jnp.eye(16, dtype=m.dtype)
    out = m[..., :, :, None, :] * eye[:, None, :, None]
    return out.reshape(*m.shape[:-3], 16 * m.shape[-2], 16 * m.shape[-1])


def ssm_weights(ab_re, ab_im, bb_re, bb_im, c):
    ab = jnp.stack([ab_re, ab_im], axis=0).reshape(2, DEPTH, 2, N_CC, 1, STATE_CH)
    ab = jnp.transpose(ab, (1, 2, 0, 3, 4, 5))
    bb = jnp.stack([bb_re, bb_im], axis=0).reshape(2, DEPTH, 2, N_CC, 16, SSM_P, SSM_N)
    bd = _block_diag(jnp.transpose(bb, (1, 2, 0, 3, 4, 5, 6))).astype(BF16)
    ct = jnp.swapaxes(c, -1, -2).reshape(DEPTH, 2, 2, N_CC, 16, SSM_N, SSM_P)
    cd = _block_diag(ct).astype(BF16)
    return ab, bd, cd


def _cmul_add(ar, ai, xr, xi, br, bi):
    return ar * xr - ai * xi + br, ar * xi + ai * xr + bi


def _ssm_kernel(u_ref, bd_ref, cd_ref, ab_ref, d_ref, h0_ref, y_ref, fin_ref,
                xr_ref, xi_ref, tr_ref, ti_ref, acc_ref, cin_ref):
    nj = BLK
    is_lat = pl.program_id(0) >= CTX_BLKS // SUB
    u = u_ref[...].reshape(nj * SUB, U_CH)
    ub = u.astype(BF16)
    acc_ref[...] = u * d_ref[...]
    for d in range(2):
        xr_ref[...] = jnp.dot(ub, bd_ref[d, 0], preferred_element_type=F32)
        xi_ref[...] = jnp.dot(ub, bd_ref[d, 1], preferred_element_type=F32)
        a1r = ab_ref[d, 0]
        a1i = ab_ref[d, 1]
        ar = jnp.broadcast_to(a1r, (SUB, STATE_CH))
        ai = jnp.broadcast_to(a1i, (SUB, STATE_CH))

        def step(i, carry, d=d, ar=ar, ai=ai):
            j = i if d == 0 else nj - 1 - i
            r0 = pl.multiple_of(j * SUB, SUB)
            nr, ni = _cmul_add(ar, ai, carry[0], carry[1],
                               xr_ref[pl.ds(r0, SUB), :], xi_ref[pl.ds(r0, SUB), :])
            xr_ref[pl.ds(r0, SUB), :] = nr
            xi_ref[pl.ds(r0, SUB), :] = ni
            return nr, ni

        zero = jnp.zeros((SUB, STATE_CH), F32)
        lax.fori_loop(0, nj, step, (zero, zero), unroll=4)

        last = (nj - 1) * SUB if d == 0 else 0
        fin_ref[d, 0] = xr_ref[last:last + SUB, :]
        fin_ref[d, 1] = xi_ref[last:last + SUB, :]

        @pl.when(is_lat)
        def _(d=d, a1r=a1r, a1i=a1i, last=last):
            def pstep(j, p):
                tr_ref[pl.ds(j, 1), :] = p[0]
                ti_ref[pl.ds(j, 1), :] = p[1]
                return _cmul_add(a1r, a1i, p[0], p[1], 0.0, 0.0)

            lax.fori_loop(0, nj, pstep, (a1r, a1i))
            anr = tr_ref[nj - 1:nj, :]
            ani = ti_ref[nj - 1:nj, :]
            cr = [None] * SUB
            ci = [None] * SUB
            order = list(range(SUB)) if d == 0 else list(range(SUB - 1, -1, -1))
            cr[order[0]] = h0_ref[d, 0]
            ci[order[0]] = h0_ref[d, 1]
            for prev, cur in zip(order[:-1], order[1:]):
                fr = xr_ref[last + prev:last + prev + 1, :]
                fi = xi_ref[last + prev:last + prev + 1, :]
                cr[cur], ci[cur] = _cmul_add(anr, ani, cr[prev], ci[prev], fr, fi)
            for s in range(SUB):
                cin_ref[0, s:s + 1, :] = cr[s]
                cin_ref[1, s:s + 1, :] = ci[s]
            cin_r = cin_ref[0]
            cin_i = cin_ref[1]

            def fstep(i, _):
                t = i if d == 0 else nj - 1 - i
                r0 = pl.multiple_of(i * SUB, SUB)
                pr = tr_ref[pl.ds(t, 1), :]
                pi = ti_ref[pl.ds(t, 1), :]
                nr, ni = _cmul_add(pr, pi, cin_r, cin_i,
                                   xr_ref[pl.ds(r0, SUB), :], xi_ref[pl.ds(r0, SUB), :])
                xr_ref[pl.ds(r0, SUB), :] = nr
                xi_ref[pl.ds(r0, SUB), :] = ni
                return 0

            lax.fori_loop(0, nj, fstep, 0, unroll=2)

        acc_ref[...] += (jnp.dot(xr_ref[...].astype(BF16), cd_ref[d, 0], preferred_element_type=F32)
                         - jnp.dot(xi_ref[...].astype(BF16), cd_ref[d, 1], preferred_element_type=F32))

    y_ref[...] = jax.nn.gelu(acc_ref[...]).reshape(nj, SUB, U_CH)


def ssm_mixer(u_tm, bd, cd, ab, dvec, h0, l):
    nrow = BLK * SUB
    sd = jax.ShapeDtypeStruct
    return pl.pallas_call(
        _ssm_kernel,
        grid=(N_SG, N_CC),
        in_specs=[pl.BlockSpec((BLK, SUB, U_CH), lambda s, c: (0, s, c)),
                  pl.BlockSpec((None, 2, 2, None, U_CH, STATE_CH), lambda s, c: (l, 0, 0, c, 0, 0)),
                  pl.BlockSpec((None, 2, 2, None, STATE_CH, U_CH), lambda s, c: (l, 0, 0, c, 0, 0)),
                  pl.BlockSpec((None, 2, 2, None, 1, STATE_CH), lambda s, c: (l, 0, 0, c, 0, 0)),
                  pl.BlockSpec((None, None, 1, U_CH), lambda s, c: (l, c, 0, 0)),
                  pl.BlockSpec((None, 2, 2, 1, STATE_CH), lambda s, c: (s, 0, 0, 0, c))],
        out_specs=[pl.BlockSpec((BLK, SUB, U_CH), lambda s, c: (0, s, c)),
                   pl.BlockSpec((2, 2, SUB, STATE_CH), lambda s, c: (0, 0, s, c))],
        out_shape=[sd((BLK, N_BLK, SSM_W), F32), sd((2, 2, N_BLK, SSM_G * SSM_N), F32)],
        scratch_shapes=[pltpu.VMEM((nrow, STATE_CH), F32), pltpu.VMEM((nrow, STATE_CH), F32),
                        pltpu.VMEM((BLK, STATE_CH), F32), pltpu.VMEM((BLK, STATE_CH), F32),
                        pltpu.VMEM((nrow, U_CH), F32), pltpu.VMEM((2, SUB, STATE_CH), F32)],
        compiler_params=_cp("parallel", "parallel"),
        name="ssm",
    )(u_tm, bd, cd, ab, dvec, h0)


CONV_HALO = 16
CONV_RC = 32


def _conv_kernel(xa_ref, xg_ref, pa_ref, pg_ref, na_ref, ng_ref, w_ref, cb_ref, lg_ref, lb_ref,
                 o_ref, xp_ref, xs_ref):
    r = pl.program_id(0)
    is_lat = r >= CTX_BLKS
    pos = jnp.bitwise_and(r, LAT_BLKS - 1)
    pv = jnp.where(is_lat & (pos != 0), 1.0, 0.0)
    nv = jnp.where(is_lat & (pos != LAT_BLKS - 1), 1.0, 0.0)
    glu = lambda a_ref, g_ref: a_ref[...] * jax.nn.sigmoid(g_ref[...])
    xp_ref[0:CONV_HALO, :] = glu(pa_ref, pg_ref) * pv
    xp_ref[CONV_HALO:CONV_HALO + BLK, :] = glu(xa_ref, xg_ref)
    xp_ref[CONV_HALO + BLK:, :] = glu(na_ref, ng_ref) * nv
    span = BLK + 3 * SUB
    for s in range(SUB):
        xs_ref[s] = xp_ref[s:s + span, :]

    def chunk(c, _):
        r0 = pl.multiple_of(c * CONV_RC, CONV_RC)
        acc = jnp.zeros((CONV_RC, CONV_CH), F32)
        for k in range(CONV_K):
            off = k + CONV_HALO - CONV_K // 2
            acc = acc + xs_ref[off % SUB, pl.ds(r0 + (off // SUB) * SUB, CONV_RC), :] * w_ref[k:k + 1, :]
        acc = acc + cb_ref[...]
        mu = jnp.mean(acc, axis=-1, keepdims=True)
        xc = acc - mu
        var = jnp.mean(xc * xc, axis=-1, keepdims=True)
        yn = xc * lax.rsqrt(var + EPS) * lg_ref[...] + lb_ref[...]
        o_ref[pl.ds(r0, CONV_RC), :] = (yn * jax.nn.sigmoid(yn)).astype(o_ref.dtype)
        return 0

    lax.fori_loop(0, BLK // CONV_RC, chunk, 0)


def conv_module(z, conv_w, conv_b, ln_g, ln_b, l):
    hb = BLK // CONV_HALO
    last_halo = T // CONV_HALO - 1
    ca, cg = 4, 5
    prev = lambda r: jnp.maximum(r * hb - 1, 0)
    nxt = lambda r: jnp.minimum((r + 1) * hb, last_halo)
    vec = lambda: pl.BlockSpec((None, 1, CONV_CH), lambda r: (l, 0, 0))
    return pl.pallas_call(
        _conv_kernel,
        grid=(N_BLK,),
        in_specs=[pl.BlockSpec((BLK, CONV_CH), lambda r: (r, ca)),
                  pl.BlockSpec((BLK, CONV_CH), lambda r: (r, cg)),
                  pl.BlockSpec((CONV_HALO, CONV_CH), lambda r: (prev(r), ca)),
                  pl.BlockSpec((CONV_HALO, CONV_CH), lambda r: (prev(r), cg)),
                  pl.BlockSpec((CONV_HALO, CONV_CH), lambda r: (nxt(r), ca)),
                  pl.BlockSpec((CONV_HALO, CONV_CH), lambda r: (nxt(r), cg)),
                  pl.BlockSpec((None, CONV_K, CONV_CH), lambda r: (l, 0, 0)),
                  vec(), vec(), vec()],
        out_specs=pl.BlockSpec((BLK, CONV_CH), lambda r: (r, 0)),
        out_shape=jax.ShapeDtypeStruct((T, CONV_CH), BF16),
        scratch_shapes=[pltpu.VMEM((BLK + 2 * CONV_HALO, CONV_CH), F32),
                        pltpu.VMEM((SUB, BLK + 3 * SUB, CONV_CH), F32)],
        compiler_params=_cp("parallel"),
        name="conv",
    )(z, z, z, z, z, z, conv_w, conv_b.reshape(DEPTH, 1, CONV_CH),
      ln_g.reshape(DEPTH, 1, CONV_CH), ln_b.reshape(DEPTH, 1, CONV_CH))


def _norm_router_kernel(x_ref, g_ref, mod_ref, rw_ref, rb_ref, h_ref, idx_ref, gate_ref):
    y = _rms(x_ref[...], g_ref[...])
    h = y * (1.0 + mod_ref[4:5, :]) + mod_ref[3:4, :]
    hb = h.astype(BF16)
    h_ref[...] = hb
    h_lo = (h - hb.astype(F32)).astype(BF16)
    w = rw_ref[...]
    wb = w.astype(BF16)
    w_lo = (w - wb.astype(F32)).astype(BF16)
    dot = lambda a, b: jnp.dot(a, b, preferred_element_type=F32)
    logits = dot(hb, wb) + (dot(hb, w_lo) + dot(h_lo, wb)) + rb_ref[...]
    lane = lax.broadcasted_iota(jnp.int32, logits.shape, 1)
    logits = jnp.where(lane < N_EXP, logits, -jnp.inf)
    col = lax.broadcasted_iota(jnp.int32, idx_ref.shape, 1)
    idx_out = jnp.zeros(idx_ref.shape, jnp.int32)
    val_out = jnp.zeros(idx_ref.shape, F32)
    cur = logits
    for k in range(TOP_K):
        mx = jnp.max(cur, axis=-1, keepdims=True)
        ix = jnp.min(jnp.where(cur == mx, lane, N_EXP), axis=-1, keepdims=True)
        idx_out = jnp.where(col == k, ix, idx_out)
        val_out = jnp.where(col == k, mx, val_out)
        cur = jnp.where(lane == ix, -jnp.inf, cur)
    gate_ref[...] = _softmax(val_out)
    idx_ref[...] = idx_out


def norm_router(x, g, mod, router_w, router_b, l):
    tm = 512
    sd = jax.ShapeDtypeStruct
    return pl.pallas_call(
        _norm_router_kernel,
        grid=(T // tm,),
        in_specs=[pl.BlockSpec((tm, D), lambda m: (m, 0)),
                  pl.BlockSpec((None, 1, D), lambda m: (l, 0, 0)),
                  pl.BlockSpec((None, None, 6, D), lambda m: (l, _grp_of_row(m * tm), 0, 0)),
                  pl.BlockSpec((None, D, LANE), lambda m: (l, 0, 0)),
                  pl.BlockSpec((None, 1, LANE), lambda m: (l, 0, 0))],
        out_specs=[pl.BlockSpec((tm, D), lambda m: (m, 0)),
                   pl.BlockSpec((tm, TOP_K), lambda m: (m, 0)),
                   pl.BlockSpec((tm, TOP_K), lambda m: (m, 0))],
        out_shape=[sd((T, D), BF16), sd((T, TOP_K), jnp.int32), sd((T, TOP_K), F32)],
        compiler_params=_cp("parallel"),
        name="norm_router",
    )(x, g.reshape(DEPTH, 1, D), mod,
      jnp.pad(router_w, ((0, 0), (0, 0), (0, LANE - N_EXP))),
      jnp.pad(router_b.reshape(DEPTH, 1, N_EXP), ((0, 0), (0, 0), (0, LANE - N_EXP))))


def _moe_kernel(be_ref, cnt_ref, nu_ref, x_ref, wg_ref, wu_ref, bg_ref, bu_ref, wo_ref, bo_ref,
                gate_ref, o_ref):
    del be_ref
    m = pl.program_id(0)
    f = pl.program_id(1)

    @pl.when(m < nu_ref[0])
    def _():
        cnt = cnt_ref[m]
        wg = wg_ref[...].astype(BF16)
        wu = wu_ref[...].astype(BF16)
        wo = wo_ref[...].astype(BF16)
        for s in range(MOE_BM // MOE_SUB):
            rows = slice(s * MOE_SUB, (s + 1) * MOE_SUB)

            @pl.when(s * MOE_SUB < cnt)
            def _(rows=rows):
                x = x_ref[rows, :]
                g = jnp.dot(x, wg, preferred_element_type=F32) + bg_ref[...]
                up = jnp.dot(x, wu, preferred_element_type=F32) + bu_ref[...]
                g = jnp.minimum(g, LIMIT)
                up = jnp.clip(up, -LIMIT, LIMIT)
                act = (up + 1.0) * (g * jax.nn.sigmoid(ALPHA * g))
                y = jnp.dot(act.astype(BF16), wo, preferred_element_type=F32)

                @pl.when(f == 0)
                def _():
                    o_ref[rows, :] = y + bo_ref[...]

                @pl.when(f > 0)
                def _():
                    o_ref[rows, :] += y

                @pl.when(f == MOE_NF - 1)
                def _():
                    o_ref[rows, :] *= gate_ref[rows, :]

            @pl.when((s * MOE_SUB >= cnt) & (f == 0))
            def _(rows=rows):
                o_ref[rows, :] = jnp.zeros((MOE_SUB, D), F32)


def moe_experts(xs, row_gate, block_e, block_cnt, n_used, w_in, b_in, w_out, b_out, l):
    def mm(m, nu):
        return jnp.minimum(m, nu[0] - 1)

    def ff(m, f, nu):
        return jnp.where(m < nu[0], f, MOE_NF - 1)

    grid_spec = pltpu.PrefetchScalarGridSpec(
        num_scalar_prefetch=3,
        grid=(MOE_BLOCKS, MOE_NF),
        in_specs=[
            pl.BlockSpec((MOE_BM, D), lambda m, f, be, cnt, nu: (mm(m, nu), 0)),
            pl.BlockSpec((None, None, D, MOE_TF),
                         lambda m, f, be, cnt, nu: (l, be[m], 0, ff(m, f, nu))),
            pl.BlockSpec((None, None, D, MOE_TF),
                         lambda m, f, be, cnt, nu: (l, be[m], 0, MOE_NF + ff(m, f, nu))),
            pl.BlockSpec((None, None, 1, MOE_TF),
                         lambda m, f, be, cnt, nu: (l, be[m], 0, ff(m, f, nu))),
            pl.BlockSpec((None, None, 1, MOE_TF),
                         lambda m, f, be, cnt, nu: (l, be[m], 0, MOE_NF + ff(m, f, nu))),
            pl.BlockSpec((None, None, MOE_TF, D),
                         lambda m, f, be, cnt, nu: (l, be[m], ff(m, f, nu), 0)),
            pl.BlockSpec((None, None, 1, D), lambda m, f, be, cnt, nu: (l, be[m], 0, 0)),
            pl.BlockSpec((MOE_BM, 1), lambda m, f, be, cnt, nu: (mm(m, nu), 0)),
        ],
        out_specs=pl.BlockSpec((MOE_BM, D), lambda m, f, be, cnt, nu: (mm(m, nu), 0)),
    )
    return pl.pallas_call(
        _moe_kernel,
        grid_spec=grid_spec,
        out_shape=jax.ShapeDtypeStruct((MOE_ROWS, D), F32),
        compiler_params=_cp("arbitrary", "arbitrary"),
        name="moe_experts",
    )(block_e, block_cnt, n_used, xs, w_in, w_in,
      b_in.reshape(DEPTH, N_EXP, 1, 2 * FF), b_in.reshape(DEPTH, N_EXP, 1, 2 * FF),
      w_out, b_out.reshape(DEPTH, N_EXP, 1, D), row_gate)


def moe_routing(top_idx, gate):
    n_assign = T * TOP_K
    flat_e = top_idx.reshape(-1)
    onehot = (flat_e[:, None] == jnp.arange(N_EXP, dtype=jnp.int32)[None, :]).astype(jnp.int32)
    csum = jnp.cumsum(onehot, axis=0)
    rank = jnp.take_along_axis(csum, flat_e[:, None], axis=1)[:, 0] - 1
    counts = csum[-1]
    nblk = (counts + MOE_BM - 1) // MOE_BM
    blk_end = jnp.cumsum(nblk)
    blk_start = blk_end - nblk
    n_used = blk_end[-1]
    dest = blk_start[flat_e] * MOE_BM + rank
    row_tok = jnp.zeros((MOE_ROWS,), jnp.int32).at[dest].set(
        jnp.arange(n_assign, dtype=jnp.int32) // TOP_K)
    row_gate = jnp.zeros((MOE_ROWS,), F32).at[dest].set(gate.reshape(-1))
    mids = jnp.minimum(jnp.arange(MOE_BLOCKS, dtype=jnp.int32), n_used - 1)
    block_e = jnp.minimum(jnp.searchsorted(blk_end, mids, side='right'), N_EXP - 1).astype(jnp.int32)
    block_cnt = jnp.clip(counts[block_e] - (mids - blk_start[block_e]) * MOE_BM, 0, MOE_BM)
    block_cnt = jnp.where(jnp.arange(MOE_BLOCKS) < n_used, block_cnt, 0).astype(jnp.int32)
    return row_tok, row_gate, dest, block_e, block_cnt, n_used.reshape(1).astype(jnp.int32)


def trunk_layer(x, l, mod, p, ssm_p, rope, cache_k, cache_v, h0):
    lambda_init = 0.8 - 0.6 * math.exp(-0.3 * l)
    zero_b = lambda n: jnp.zeros((DEPTH, 1, n), F32)

    h = norm_mod(x, p['norm_mix'], mod, l)
    z = matmul(h, p['w_in'], zero_b(IN_W), l, name="w_in")

    att = attn_ctx(z, p['diff_lambda'], p['diff_subln'], l, lambda_init)
    att = attn_lat(z, cache_k, cache_v, rope[0], rope[1], p['diff_lambda'], p['diff_subln'],
                   att, l, lambda_init)
    att_out = matmul(att, p['w_attn_out'], zero_b(D), l, name="attn_out")

    ab, bd, cd = ssm_p
    u_tm = jnp.transpose(z.reshape(N_BLK, BLK, IN_W)[:, :, 3 * QKV_W:3 * QKV_W + SSM_W], (1, 0, 2))
    y_tm, fin = ssm_mixer(u_tm, bd, cd, ab, p['ssm_d'].reshape(DEPTH, N_CC, 1, U_CH), h0, l)
    y = jnp.transpose(y_tm, (1, 0, 2)).reshape(T, SSM_W).astype(BF16)
    ssm_out = matmul_glu(y, p['w_ssm_glu'], l)

    cnv = conv_module(z, p['conv_w'], p['conv_b'], p['conv_ln_g'], p['conv_ln_b'], l)
    conv_out = matmul(cnv, p['w_conv_out'], p['b_conv_out'].reshape(DEPTH, 1, D), l, name="conv_out")

    mixed = merge_gates(h, p['w_merge_gate'], p['b_merge_gate'].reshape(DEPTH, 1, 3 * D),
                        att_out, ssm_out, conv_out, l)
    x = matmul_resid(mixed, p['w_out'], x, mod, l, gate_row=2)

    h2, top_idx, gate = norm_router(x, p['norm_ffn'], mod, p['router_w'], p['router_b'], l)
    row_tok, row_gate, dest, block_e, block_cnt, n_used = moe_routing(top_idx, gate)
    xs = jnp.take(h2, row_tok, axis=0)
    ys = moe_experts(xs, row_gate.reshape(MOE_ROWS, 1), block_e, block_cnt, n_used,
                     p['moe_w_in'], p['moe_b_in'], p['moe_w_out'], p['moe_b_out'], l)
    ffn = jnp.take(ys, dest, axis=0).reshape(T, TOP_K, D).sum(axis=1)
    g2 = jnp.concatenate([jnp.broadcast_to(mod[l, g, 5], (n, D))
                          for g, n in ((0, T_CTX), (1, DEC_SEQ), (2, DEC_SEQ))], axis=0)
    x = x + g2 * ffn

    k_new = z[:T_CTX, QKV_W:2 * QKV_W].reshape(BATCH, SEQ, 2 * HEADS, HEAD_DIM)
    v_new = z[:T_CTX, 2 * QKV_W:3 * QKV_W].reshape(BATCH, SEQ, HEADS, 2 * HEAD_DIM)
    s_new = jnp.transpose(fin[:, :, :BATCH, :], (2, 0, 1, 3)).reshape(BATCH, 2, 2, SSM_G, SSM_N)
    return x, k_new, v_new, s_new


def kernel(x_prompt, x_sample, cache_k, cache_v, state_ssm, c, c_ctx, w_ada, b_ada, norm_mix, norm_ffn, w_in, diff_lambda, diff_subln, w_attn_out, ssm_a_re, ssm_a_im, ssm_log_dt, ssm_b, ssm_c, ssm_d, w_ssm_glu, conv_w, conv_b, conv_ln_g, conv_ln_b, w_conv_out, b_conv_out, w_merge_gate, b_merge_gate, w_out, router_w, router_b, moe_w_in, moe_b_in, moe_w_out, moe_b_out, norm_final):
    p = dict(norm_mix=norm_mix, norm_ffn=norm_ffn, w_in=w_in, diff_lambda=diff_lambda,
             diff_subln=diff_subln, w_attn_out=w_attn_out, ssm_d=ssm_d, w_ssm_glu=w_ssm_glu,
             conv_w=conv_w, conv_b=conv_b, conv_ln_g=conv_ln_g, conv_ln_b=conv_ln_b,
             w_conv_out=w_conv_out, b_conv_out=b_conv_out, w_merge_gate=w_merge_gate,
             b_merge_gate=b_merge_gate, w_out=w_out, router_w=router_w, router_b=router_b,
             moe_w_in=moe_w_in, moe_b_in=moe_b_in, moe_w_out=moe_w_out, moe_b_out=moe_b_out)

    cond8 = jnp.zeros((SUB, D), F32).at[0].set(c_ctx).at[1:1 + DEC_BATCH].set(c)
    mod = ada_all(cond8, w_ada, b_ada).reshape(DEPTH, SUB, 6, D)

    ab_re, ab_im, bb_re, bb_im = ssm_discretize(ssm_a_re, ssm_a_im, ssm_log_dt, ssm_b)
    ssm_p = ssm_weights(ab_re, ab_im, bb_re, bb_im, ssm_c)
    rope = rope_tables()

    ck = cache_k.reshape(DEC_BATCH, DEPTH, PAST, QKV_W)
    cv = cache_v.reshape(DEC_BATCH, DEPTH, PAST, QKV_W)
    st = state_ssm.reshape(DEC_BATCH, DEPTH, 2, 2, 1, SSM_G * SSM_N)
    h0_all = jnp.concatenate([jnp.zeros((N_SG - DEC_BATCH,) + st.shape[1:], F32), st], axis=0)

    x = jnp.concatenate([x_prompt.reshape(T_CTX, D), x_sample.reshape(T_LAT, D)], axis=0)
    new_k, new_v, new_s = [], [], []
    for l in range(DEPTH):
        x, k_l, v_l, s_l = trunk_layer(x, l, mod, p, ssm_p, rope, ck, cv, h0_all[:, l])
        new_k.append(k_l)
        new_v.append(v_l)
        new_s.append(s_l)
    y = final_norm(x, norm_final)
    return (y[:T_CTX].reshape(BATCH, SEQ, D), y[T_CTX:].reshape(DEC_BATCH, DEC_SEQ, D),
            jnp.stack(new_k, axis=1), jnp.stack(new_v, axis=1), jnp.stack(new_s, axis=1))
```

```python
import functools
import math

import jax
import jax.numpy as jnp
from jax import lax
from jax.experimental import pallas as pl
from jax.experimental.pallas import tpu as pltpu

F32 = jnp.float32
BF16 = jnp.bfloat16

D = 2048
BATCH = 32
SEQ = 256
DEPTH = 4
DEC_BATCH = 2
DEC_SEQ = 2048
PAST = 512
GRID_W = 64
HEADS = 8
HEAD_DIM = 64
ROPE_PAIRS = 16
ROPE_BASE = 10000.0
QKV_W = 1024
SSM_W = 1024
SSM_G = 64
SSM_P = 16
SSM_N = 64
CONV_CH = 1024
CONV_K = 31
IN_W = 6144
N_EXP = 32
TOP_K = 4
FF = 2048
LIMIT = 7.0
ALPHA = 1.702
EPS = 1e-6

T_CTX = BATCH * SEQ
T_LAT = DEC_BATCH * DEC_SEQ
T = T_CTX + T_LAT
BLK = 256
N_BLK = T // BLK
CTX_BLKS = T_CTX // BLK
LAT_BLKS = DEC_SEQ // BLK
N_GRP = 1 + DEC_BATCH
SUB = 8
LANE = 128
STATE_CH = 1024
U_CH = 256
N_CC = SSM_W // U_CH
N_SG = N_BLK // SUB

MOE_BM = 1024
MOE_SUB = 256
MOE_TF = 256
MOE_NF = FF // MOE_TF
MOE_TN = 512
MOE_NN = D // MOE_TN
MOE_BLOCKS = (T * TOP_K) // MOE_BM + N_EXP
MOE_ROWS = MOE_BLOCKS * MOE_BM

VMEM_LIMIT = 56 << 20


def _cp(*sem, vmem=VMEM_LIMIT):
    return pltpu.CompilerParams(dimension_semantics=sem, vmem_limit_bytes=vmem)


def _grp_of_row(row):
    return jnp.where(row < T_CTX, 0, 1 + (row - T_CTX) // DEC_SEQ)


def _ada_kernel(c_ref, w_ref, b_ref, o_ref):
    c = c_ref[...]
    a = (c * jax.nn.sigmoid(c)).astype(BF16)
    o_ref[...] = jnp.dot(a, w_ref[...].astype(BF16), preferred_element_type=F32) + b_ref[...]


def ada_all(cond8, w_ada, b_ada):
    tn = 1024
    n = 6 * D
    return pl.pallas_call(
        _ada_kernel,
        grid=(DEPTH, n // tn),
        in_specs=[pl.BlockSpec((SUB, D), lambda l, j: (0, 0)),
                  pl.BlockSpec((None, D, tn), lambda l, j: (l, 0, j)),
                  pl.BlockSpec((None, 1, tn), lambda l, j: (l, 0, j))],
        out_specs=pl.BlockSpec((None, SUB, tn), lambda l, j: (l, 0, j)),
        out_shape=jax.ShapeDtypeStruct((DEPTH, SUB, n), F32),
        compiler_params=_cp("parallel", "parallel"),
        name="ada",
    )(cond8, w_ada, b_ada.reshape(DEPTH, 1, n))


def _rms(x, g):
    return x * lax.rsqrt(jnp.mean(x * x, axis=-1, keepdims=True) + EPS) * g


def _norm_mod_kernel(x_ref, g_ref, mod_ref, o_ref):
    y = _rms(x_ref[...], g_ref[...])
    sh = mod_ref[0:1, :]
    sc = mod_ref[1:2, :]
    o_ref[...] = (y * (1.0 + sc) + sh).astype(o_ref.dtype)


def norm_mod(x, g, mod, l):
    tm = 512
    return pl.pallas_call(
        _norm_mod_kernel,
        grid=(T // tm,),
        in_specs=[pl.BlockSpec((tm, D), lambda m: (m, 0)),
                  pl.BlockSpec((None, 1, D), lambda m: (l, 0, 0)),
                  pl.BlockSpec((None, None, 6, D), lambda m: (l, _grp_of_row(m * tm), 0, 0))],
        out_specs=pl.BlockSpec((tm, D), lambda m: (m, 0)),
        out_shape=jax.ShapeDtypeStruct((T, D), BF16),
        compiler_params=_cp("parallel"),
        name="norm_mod",
    )(x, g.reshape(DEPTH, 1, D), mod)


def _final_norm_kernel(x_ref, g_ref, o_ref):
    o_ref[...] = _rms(x_ref[...], g_ref[...])


def final_norm(x, g):
    tm = 512
    return pl.pallas_call(
        _final_norm_kernel,
        grid=(T // tm,),
        in_specs=[pl.BlockSpec((tm, D), lambda m: (m, 0)),
                  pl.BlockSpec((1, D), lambda m: (0, 0))],
        out_specs=pl.BlockSpec((tm, D), lambda m: (m, 0)),
        out_shape=jax.ShapeDtypeStruct((T, D), F32),
        compiler_params=_cp("parallel"),
        name="final_norm",
    )(x, g.reshape(1, D))


def _mm_kernel(a_ref, w_ref, b_ref, o_ref, wb_ref):
    @pl.when(pl.program_id(1) == 0)
    def _():
        wb_ref[...] = w_ref[...].astype(BF16)

    acc = jnp.dot(a_ref[...], wb_ref[...], preferred_element_type=F32)
    o_ref[...] = (acc + b_ref[...]).astype(o_ref.dtype)


def matmul(a, w, bias, l, *, tm=1024, tn=1024, out_dtype=F32, name="mm"):
    k, n = w.shape[1], w.shape[2]
    return pl.pallas_call(
        _mm_kernel,
        grid=(n // tn, T // tm),
        in_specs=[pl.BlockSpec((tm, k), lambda j, m: (m, 0)),
                  pl.BlockSpec((None, k, tn), lambda j, m: (l, 0, j)),
                  pl.BlockSpec((None, 1, tn), lambda j, m: (l, 0, j))],
        out_specs=pl.BlockSpec((tm, tn), lambda j, m: (m, j)),
        out_shape=jax.ShapeDtypeStruct((T, n), out_dtype),
        scratch_shapes=[pltpu.VMEM((k, tn), BF16)],
        compiler_params=_cp("parallel", "arbitrary"),
        name=name,
    )(a, w, bias)


def _mm_glu_kernel(a_ref, wa_ref, wg_ref, o_ref, wab_ref, wgb_ref):
    @pl.when(pl.program_id(1) == 0)
    def _():
        wab_ref[...] = wa_ref[...].astype(BF16)
        wgb_ref[...] = wg_ref[...].astype(BF16)

    a = a_ref[...]
    va = jnp.dot(a, wab_ref[...], preferred_element_type=F32)
    vg = jnp.dot(a, wgb_ref[...], preferred_element_type=F32)
    o_ref[...] = va * jax.nn.sigmoid(vg)


def matmul_glu(a, w, l, *, tm=1024, tn=512):
    k, n = w.shape[1], w.shape[2] // 2
    nj = n // tn
    return pl.pallas_call(
        _mm_glu_kernel,
        grid=(nj, T // tm),
        in_specs=[pl.BlockSpec((tm, k), lambda j, m: (m, 0)),
                  pl.BlockSpec((None, k, tn), lambda j, m: (l, 0, j)),
                  pl.BlockSpec((None, k, tn), lambda j, m: (l, 0, nj + j))],
        out_specs=pl.BlockSpec((tm, tn), lambda j, m: (m, j)),
        out_shape=jax.ShapeDtypeStruct((T, n), F32),
        scratch_shapes=[pltpu.VMEM((k, tn), BF16), pltpu.VMEM((k, tn), BF16)],
        compiler_params=_cp("parallel", "arbitrary"),
        name="ssm_glu",
    )(a, w, w)


def _mm_resid_kernel(a_ref, w_ref, x_ref, mod_ref, o_ref, wb_ref, *, gate_row):
    @pl.when(pl.program_id(1) == 0)
    def _():
        wb_ref[...] = w_ref[...].astype(BF16)

    acc = jnp.dot(a_ref[...], wb_ref[...], preferred_element_type=F32)
    o_ref[...] = x_ref[...] + mod_ref[gate_row:gate_row + 1, :] * acc


def matmul_resid(a, w, x, mod, l, *, gate_row, tm=1024, tn=1024):
    k, n = w.shape[1], w.shape[2]
    return pl.pallas_call(
        functools.partial(_mm_resid_kernel, gate_row=gate_row),
        grid=(n // tn, T // tm),
        in_specs=[pl.BlockSpec((tm, k), lambda j, m: (m, 0)),
                  pl.BlockSpec((None, k, tn), lambda j, m: (l, 0, j)),
                  pl.BlockSpec((tm, tn), lambda j, m: (m, j)),
                  pl.BlockSpec((None, None, 6, tn), lambda j, m: (l, _grp_of_row(m * tm), 0, j))],
        out_specs=pl.BlockSpec((tm, tn), lambda j, m: (m, j)),
        out_shape=jax.ShapeDtypeStruct((T, n), F32),
        scratch_shapes=[pltpu.VMEM((k, tn), BF16)],
        compiler_params=_cp("parallel", "arbitrary"),
        name="out_resid",
    )(a, w, x, mod)


def _merge_kernel(h_ref, w0_ref, w1_ref, w2_ref, b0_ref, b1_ref, b2_ref,
                  att_ref, ssm_ref, cnv_ref, o_ref, wb0_ref, wb1_ref, wb2_ref):
    @pl.when(pl.program_id(1) == 0)
    def _():
        wb0_ref[...] = w0_ref[...].astype(BF16)
        wb1_ref[...] = w1_ref[...].astype(BF16)
        wb2_ref[...] = w2_ref[...].astype(BF16)

    h = h_ref[...]

    def gate(wb_ref, b_ref):
        return jax.nn.sigmoid(jnp.dot(h, wb_ref[...], preferred_element_type=F32) + b_ref[...])

    mixed = (gate(wb0_ref, b0_ref) * att_ref[...] + gate(wb1_ref, b1_ref) * ssm_ref[...]
             + gate(wb2_ref, b2_ref) * cnv_ref[...])
    o_ref[...] = mixed.astype(o_ref.dtype)


def merge_gates(h, w, b, att, ssm, cnv, l, *, tm=512, tn=512):
    nj = D // tn
    wspec = lambda br: pl.BlockSpec((None, D, tn), lambda j, m: (l, 0, br * nj + j))
    bspec = lambda br: pl.BlockSpec((None, 1, tn), lambda j, m: (l, 0, br * nj + j))
    tile = pl.BlockSpec((tm, tn), lambda j, m: (m, j))
    return pl.pallas_call(
        _merge_kernel,
        grid=(nj, T // tm),
        in_specs=[pl.BlockSpec((tm, D), lambda j, m: (m, 0)),
                  wspec(0), wspec(1), wspec(2), bspec(0), bspec(1), bspec(2),
                  tile, tile, tile],
        out_specs=tile,
        out_shape=jax.ShapeDtypeStruct((T, D), BF16),
        scratch_shapes=[pltpu.VMEM((D, tn), BF16)] * 3,
        compiler_params=_cp("parallel", "arbitrary"),
        name="merge_gates",
    )(h, w, w, w, b, b, b, att, ssm, cnv)


def _lambda(lam_ref, lambda_init):
    lp = lam_ref[...]
    s01 = jnp.sum(lp[0:1, :] * lp[1:2, :], axis=-1, keepdims=True)
    s23 = jnp.sum(lp[2:3, :] * lp[3:4, :], axis=-1, keepdims=True)
    return jnp.exp(s01) - jnp.exp(s23) + lambda_init


def _softmax(s):
    e = jnp.exp(s - jnp.max(s, axis=-1, keepdims=True))
    return e * (1.0 / jnp.sum(e, axis=-1, keepdims=True))


def _diff_attn_head(q, kb, vb, lam, g, lambda_init):
    lane = lax.broadcasted_iota(jnp.int32, q.shape, 1)
    q_a = jnp.where(lane < HEAD_DIM, q, 0.0).astype(BF16)
    q_b = jnp.where(lane >= HEAD_DIM, q, 0.0).astype(BF16)
    nt = (((1,), (1,)), ((), ()))
    p_a = _softmax(lax.dot_general(q_a, kb, nt, preferred_element_type=F32))
    p_b = _softmax(lax.dot_general(q_b, kb, nt, preferred_element_type=F32))
    a = (p_a - lam * p_b).astype(BF16)
    o = jnp.dot(a, vb, preferred_element_type=F32)
    return _rms(o, g) * (1.0 - lambda_init)


def _attn_ctx_kernel(q_ref, k_ref, v_ref, lam_ref, g_ref, o_ref, *, lambda_init):
    lam = _lambda(lam_ref, lambda_init)
    g = g_ref[...]
    for h in range(HEADS):
        cols = slice(h * LANE, (h + 1) * LANE)
        q = q_ref[:, cols] * (HEAD_DIM ** -0.5)
        o = _diff_attn_head(q, k_ref[:, cols].astype(BF16), v_ref[:, cols].astype(BF16),
                            lam, g, lambda_init)
        o_ref[:, cols] = o.astype(o_ref.dtype)


def attn_ctx(z, lam_p, subln, l, lambda_init):
    return pl.pallas_call(
        functools.partial(_attn_ctx_kernel, lambda_init=lambda_init),
        grid=(BATCH,),
        in_specs=[pl.BlockSpec((SEQ, QKV_W), lambda b: (b, 0)),
                  pl.BlockSpec((SEQ, QKV_W), lambda b: (b, 1)),
                  pl.BlockSpec((SEQ, QKV_W), lambda b: (b, 2)),
                  pl.BlockSpec((None, 4, HEAD_DIM), lambda b: (l, 0, 0)),
                  pl.BlockSpec((None, 1, LANE), lambda b: (l, 0, 0))],
        out_specs=pl.BlockSpec((SEQ, QKV_W), lambda b: (b, 0)),
        out_shape=jax.ShapeDtypeStruct((T, QKV_W), BF16),
        compiler_params=_cp("parallel"),
        name="attn_ctx",
    )(z, z, z, lam_p, subln.reshape(DEPTH, 1, LANE))


def _rope(x, cos, sin_signed):
    lane = lax.broadcasted_iota(jnp.int32, x.shape, 1)
    first = (lane % (2 * ROPE_PAIRS)) < ROPE_PAIRS
    partner = jnp.where(first, pltpu.roll(x, LANE - ROPE_PAIRS, 1), pltpu.roll(x, ROPE_PAIRS, 1))
    return x * cos + partner * sin_signed


def _attn_lat_kernel(q_ref, k_ref, v_ref, ck_ref, cv_ref, cq_ref, sq_ref, ckk_ref, skk_ref,
                     lam_ref, g_ref, prev_ref, o_ref, kb_ref, vb_ref, *, lambda_init):
    del prev_ref

    @pl.when(pl.program_id(2) == 0)
    def _():
        kb_ref[0:DEC_SEQ, :] = _rope(k_ref[...], ckk_ref[...], skk_ref[...]).astype(BF16)
        kb_ref[DEC_SEQ:DEC_SEQ + PAST, :] = ck_ref[...].astype(BF16)
        vb_ref[0:DEC_SEQ, :] = v_ref[...].astype(BF16)
        vb_ref[DEC_SEQ:DEC_SEQ + PAST, :] = cv_ref[...].astype(BF16)

    q = _rope(q_ref[...], cq_ref[...], sq_ref[...]) * (HEAD_DIM ** -0.5)
    o = _diff_attn_head(q, kb_ref[...], vb_ref[...], _lambda(lam_ref, lambda_init), g_ref[...],
                        lambda_init)
    o_ref[...] = o.astype(o_ref.dtype)


def attn_lat(z, cache_k, cache_v, cos_t, sin_t, lam_p, subln, prev, l, lambda_init):
    tq = BLK
    nq = DEC_SEQ // tq
    qrow = lambda b, h, i: CTX_BLKS + b * nq + i
    krow = lambda b: T_CTX // DEC_SEQ + b
    return pl.pallas_call(
        functools.partial(_attn_lat_kernel, lambda_init=lambda_init),
        grid=(DEC_BATCH, HEADS, nq),
        in_specs=[pl.BlockSpec((tq, LANE), lambda b, h, i: (qrow(b, h, i), h)),
                  pl.BlockSpec((DEC_SEQ, LANE), lambda b, h, i: (krow(b), HEADS + h)),
                  pl.BlockSpec((DEC_SEQ, LANE), lambda b, h, i: (krow(b), 2 * HEADS + h)),
                  pl.BlockSpec((None, None, PAST, LANE), lambda b, h, i: (b, l, 0, h)),
                  pl.BlockSpec((None, None, PAST, LANE), lambda b, h, i: (b, l, 0, h)),
                  pl.BlockSpec((tq, LANE), lambda b, h, i: (i, 0)),
                  pl.BlockSpec((tq, LANE), lambda b, h, i: (i, 0)),
                  pl.BlockSpec((DEC_SEQ, LANE), lambda b, h, i: (0, 0)),
                  pl.BlockSpec((DEC_SEQ, LANE), lambda b, h, i: (0, 0)),
                  pl.BlockSpec((None, 4, HEAD_DIM), lambda b, h, i: (l, 0, 0)),
                  pl.BlockSpec((None, 1, LANE), lambda b, h, i: (l, 0, 0)),
                  pl.BlockSpec(memory_space=pl.ANY)],
        out_specs=pl.BlockSpec((tq, LANE), lambda b, h, i: (qrow(b, h, i), h)),
        out_shape=jax.ShapeDtypeStruct((T, QKV_W), BF16),
        scratch_shapes=[pltpu.VMEM((DEC_SEQ + PAST, LANE), BF16)] * 2,
        input_output_aliases={11: 0},
        compiler_params=_cp("parallel", "parallel", "arbitrary"),
        name="attn_lat",
    )(z, z, z, cache_k, cache_v, cos_t, sin_t, cos_t, sin_t, lam_p,
      subln.reshape(DEPTH, 1, LANE), prev)


def rope_tables():
    row = (jnp.arange(DEC_SEQ) // GRID_W).astype(F32)
    col = (jnp.arange(DEC_SEQ) % GRID_W).astype(F32)
    inv_freq = ROPE_BASE ** (-jnp.arange(ROPE_PAIRS, dtype=F32) / ROPE_PAIRS)
    ang = jnp.stack([row[:, None] * inv_freq, col[:, None] * inv_freq], axis=1)
    cos, sin = jnp.cos(ang), jnp.sin(ang)
    cos64 = jnp.concatenate([cos, cos], axis=-1).reshape(DEC_SEQ, HEAD_DIM)
    sin64 = jnp.concatenate([-sin, sin], axis=-1).reshape(DEC_SEQ, HEAD_DIM)
    return jnp.tile(cos64, (1, 2)), jnp.tile(sin64, (1, 2))


def _zoh_kernel(are_ref, aim_ref, ldt_ref, bre_ref, bim_ref,
                abr_ref, abi_ref, bbr_ref, bbi_ref):
    a_re = are_ref[...]
    a_im = aim_ref[...]
    dt = jnp.exp(ldt_ref[...])
    mag = jnp.exp(dt * a_re)
    ang = dt * a_im
    ab_re = mag * jnp.cos(ang)
    ab_im = mag * jnp.sin(ang)
    den = a_re * a_re + a_im * a_im
    xm = ab_re - 1.0
    f_re = (xm * a_re + ab_im * a_im) / den
    f_im = (ab_im * a_re - xm * a_im) / den
    abr_ref[...] = ab_re
    abi_ref[...] = ab_im
    rows = a_re.shape[0]
    fr = jnp.broadcast_to(f_re[:, None, :], (rows, SSM_P, SSM_N))
    fi = jnp.broadcast_to(f_im[:, None, :], (rows, SSM_P, SSM_N))
    b_re = bre_ref[...]
    b_im = bim_ref[...]
    bbr_ref[...] = fr * b_re - fi * b_im
    bbi_ref[...] = fr * b_im + fi * b_re


def ssm_discretize(a_re, a_im, log_dt, b):
    rows = DEPTH * 2 * SSM_G
    bt = jnp.swapaxes(b, -1, -2)
    b_re = bt[:, :, 0].reshape(rows, SSM_P, SSM_N)
    b_im = bt[:, :, 1].reshape(rows, SSM_P, SSM_N)
    sd = jax.ShapeDtypeStruct
    ab_re, ab_im, bb_re, bb_im = pl.pallas_call(
        _zoh_kernel,
        out_shape=[sd((rows, SSM_N), F32), sd((rows, SSM_N), F32),
                   sd((rows, SSM_P, SSM_N), F32), sd((rows, SSM_P, SSM_N), F32)],
        compiler_params=pltpu.CompilerParams(vmem_limit_bytes=VMEM_LIMIT),
        name="ssm_zoh",
    )(a_re.reshape(rows, SSM_N), a_im.reshape(rows, SSM_N), log_dt.reshape(rows, 1), b_re, b_im)
    return ab_re, ab_im, bb_re, bb_im


def _block_diag(m):
    eye = jnp.eye(16, dtype=m.dtype)
    out = m[..., :, :, None, :] * eye[:, None, :, None]
    return out.reshape(*m.shape[:-3], 16 * m.shape[-2], 16 * m.shape[-1])


def ssm_weights(ab_re, ab_im, bb_re, bb_im, c):
    ab = jnp.stack([ab_re, ab_im], axis=0).reshape(2, DEPTH, 2, N_CC, 1, STATE_CH)
    ab = jnp.transpose(ab, (1, 2, 0, 3, 4, 5))
    bb = jnp.stack([bb_re, bb_im], axis=0).reshape(2, DEPTH, 2, N_CC, 16, SSM_P, SSM_N)
    bd = _block_diag(jnp.transpose(bb, (1, 2, 0, 3, 4, 5, 6))).astype(BF16)
    ct = jnp.swapaxes(c, -1, -2).reshape(DEPTH, 2, 2, N_CC, 16, SSM_N, SSM_P)
    cd = _block_diag(ct).astype(BF16)
    return ab, bd, cd


def _cmul_add(ar, ai, xr, xi, br, bi):
    return ar * xr - ai * xi + br, ar * xi + ai * xr + bi


def _ssm_kernel(u_ref, bd_ref, cd_ref, ab_ref, d_ref, h0_ref, y_ref, fin_ref,
                xr_ref, xi_ref, tr_ref, ti_ref, acc_ref, cin_ref):
    nj = BLK
    is_lat = pl.program_id(0) >= CTX_BLKS // SUB
    u = u_ref[...].reshape(nj * SUB, U_CH)
    ub = u.astype(BF16)
    acc_ref[...] = u * d_ref[...]
    for d in range(2):
        xr_ref[...] = jnp.dot(ub, bd_ref[d, 0], preferred_element_type=F32)
        xi_ref[...] = jnp.dot(ub, bd_ref[d, 1], preferred_element_type=F32)
        a1r = ab_ref[d, 0]
        a1i = ab_ref[d, 1]
        ar = jnp.broadcast_to(a1r, (SUB, STATE_CH))
        ai = jnp.broadcast_to(a1i, (SUB, STATE_CH))

        def step(i, carry, d=d, ar=ar, ai=ai):
            j = i if d == 0 else nj - 1 - i
            r0 = pl.multiple_of(j * SUB, SUB)
            nr, ni = _cmul_add(ar, ai, carry[0], carry[1],
                               xr_ref[pl.ds(r0, SUB), :], xi_ref[pl.ds(r0, SUB), :])
            xr_ref[pl.ds(r0, SUB), :] = nr
            xi_ref[pl.ds(r0, SUB), :] = ni
            return nr, ni

        zero = jnp.zeros((SUB, STATE_CH), F32)
        lax.fori_loop(0, nj, step, (zero, zero), unroll=4)

        last = (nj - 1) * SUB if d == 0 else 0
        fin_ref[d, 0] = xr_ref[last:last + SUB, :]
        fin_ref[d, 1] = xi_ref[last:last + SUB, :]

        @pl.when(is_lat)
        def _(d=d, a1r=a1r, a1i=a1i, last=last):
            def pstep(j, p):
                tr_ref[pl.ds(j, 1), :] = p[0]
                ti_ref[pl.ds(j, 1), :] = p[1]
                return _cmul_add(a1r, a1i, p[0], p[1], 0.0, 0.0)

            lax.fori_loop(0, nj, pstep, (a1r, a1i))
            anr = tr_ref[nj - 1:nj, :]
            ani = ti_ref[nj - 1:nj, :]
            cr = [None] * SUB
            ci = [None] * SUB
            order = list(range(SUB)) if d == 0 else list(range(SUB - 1, -1, -1))
            cr[order[0]] = h0_ref[d, 0]
            ci[order[0]] = h0_ref[d, 1]
            for prev, cur in zip(order[:-1], order[1:]):
                fr = xr_ref[last + prev:last + prev + 1, :]
                fi = xi_ref[last + prev:last + prev + 1, :]
                cr[cur], ci[cur] = _cmul_add(anr, ani, cr[prev], ci[prev], fr, fi)
            for s in range(SUB):
                cin_ref[0, s:s + 1, :] = cr[s]
                cin_ref[1, s:s + 1, :] = ci[s]
            cin_r = cin_ref[0]
            cin_i = cin_ref[1]

            def fstep(i, _):
                t = i if d == 0 else nj - 1 - i
                r0 = pl.multiple_of(i * SUB, SUB)
                pr = tr_ref[pl.ds(t, 1), :]
                pi = ti_ref[pl.ds(t, 1), :]
                nr, ni = _cmul_add(pr, pi, cin_r, cin_i,
                                   xr_ref[pl.ds(r0, SUB), :], xi_ref[pl.ds(r0, SUB), :])
                xr_ref[pl.ds(r0, SUB), :] = nr
                xi_ref[pl.ds(r0, SUB), :] = ni
                return 0

            lax.fori_loop(0, nj, fstep, 0, unroll=2)

        acc_ref[...] += (jnp.dot(xr_ref[...].astype(BF16), cd_ref[d, 0], preferred_element_type=F32)
                         - jnp.dot(xi_ref[...].astype(BF16), cd_ref[d, 1], preferred_element_type=F32))

    y_ref[...] = jax.nn.gelu(acc_ref[...]).reshape(nj, SUB, U_CH)


def ssm_mixer(u_tm, bd, cd, ab, dvec, h0, l):
    nrow = BLK * SUB
    sd = jax.ShapeDtypeStruct
    return pl.pallas_call(
        _ssm_kernel,
        grid=(N_SG, N_CC),
        in_specs=[pl.BlockSpec((BLK, SUB, U_CH), lambda s, c: (0, s, c)),
                  pl.BlockSpec((None, 2, 2, None, U_CH, STATE_CH), lambda s, c: (l, 0, 0, c, 0, 0)),
                  pl.BlockSpec((None, 2, 2, None, STATE_CH, U_CH), lambda s, c: (l, 0, 0, c, 0, 0)),
                  pl.BlockSpec((None, 2, 2, None, 1, STATE_CH), lambda s, c: (l, 0, 0, c, 0, 0)),
                  pl.BlockSpec((None, None, 1, U_CH), lambda s, c: (l, c, 0, 0)),
                  pl.BlockSpec((None, 2, 2, 1, STATE_CH), lambda s, c: (s, 0, 0, 0, c))],
        out_specs=[pl.BlockSpec((BLK, SUB, U_CH), lambda s, c: (0, s, c)),
                   pl.BlockSpec((2, 2, SUB, STATE_CH), lambda s, c: (0, 0, s, c))],
        out_shape=[sd((BLK, N_BLK, SSM_W), F32), sd((2, 2, N_BLK, SSM_G * SSM_N), F32)],
        scratch_shapes=[pltpu.VMEM((nrow, STATE_CH), F32), pltpu.VMEM((nrow, STATE_CH), F32),
                        pltpu.VMEM((BLK, STATE_CH), F32), pltpu.VMEM((BLK, STATE_CH), F32),
                        pltpu.VMEM((nrow, U_CH), F32), pltpu.VMEM((2, SUB, STATE_CH), F32)],
        compiler_params=_cp("parallel", "parallel"),
        name="ssm",
    )(u_tm, bd, cd, ab, dvec, h0)


CONV_HALO = 16
CONV_RC = 32


def _conv_kernel(xa_ref, xg_ref, pa_ref, pg_ref, na_ref, ng_ref, w_ref, cb_ref, lg_ref, lb_ref,
                 o_ref, xp_ref, xs_ref):
    r = pl.program_id(0)
    is_lat = r >= CTX_BLKS
    pos = jnp.bitwise_and(r, LAT_BLKS - 1)
    pv = jnp.where(is_lat & (pos != 0), 1.0, 0.0)
    nv = jnp.where(is_lat & (pos != LAT_BLKS - 1), 1.0, 0.0)
    glu = lambda a_ref, g_ref: a_ref[...] * jax.nn.sigmoid(g_ref[...])
    xp_ref[0:CONV_HALO, :] = glu(pa_ref, pg_ref) * pv
    xp_ref[CONV_HALO:CONV_HALO + BLK, :] = glu(xa_ref, xg_ref)
    xp_ref[CONV_HALO + BLK:, :] = glu(na_ref, ng_ref) * nv
    span = BLK + 3 * SUB
    for s in range(SUB):
        xs_ref[s] = xp_ref[s:s + span, :]

    def chunk(c, _):
        r0 = pl.multiple_of(c * CONV_RC, CONV_RC)
        acc = jnp.zeros((CONV_RC, CONV_CH), F32)
        for k in range(CONV_K):
            off = k + CONV_HALO - CONV_K // 2
            acc = acc + xs_ref[off % SUB, pl.ds(r0 + (off // SUB) * SUB, CONV_RC), :] * w_ref[k:k + 1, :]
        acc = acc + cb_ref[...]
        mu = jnp.mean(acc, axis=-1, keepdims=True)
        xc = acc - mu
        var = jnp.mean(xc * xc, axis=-1, keepdims=True)
        yn = xc * lax.rsqrt(var + EPS) * lg_ref[...] + lb_ref[...]
        o_ref[pl.ds(r0, CONV_RC), :] = (yn * jax.nn.sigmoid(yn)).astype(o_ref.dtype)
        return 0

    lax.fori_loop(0, BLK // CONV_RC, chunk, 0)


def conv_module(z, conv_w, conv_b, ln_g, ln_b, l):
    hb = BLK // CONV_HALO
    last_halo = T // CONV_HALO - 1
    ca, cg = 4, 5
    prev = lambda r: jnp.maximum(r * hb - 1, 0)
    nxt = lambda r: jnp.minimum((r + 1) * hb, last_halo)
    vec = lambda: pl.BlockSpec((None, 1, CONV_CH), lambda r: (l, 0, 0))
    return pl.pallas_call(
        _conv_kernel,
        grid=(N_BLK,),
        in_specs=[pl.BlockSpec((BLK, CONV_CH), lambda r: (r, ca)),
                  pl.BlockSpec((BLK, CONV_CH), lambda r: (r, cg)),
                  pl.BlockSpec((CONV_HALO, CONV_CH), lambda r: (prev(r), ca)),
                  pl.BlockSpec((CONV_HALO, CONV_CH), lambda r: (prev(r), cg)),
                  pl.BlockSpec((CONV_HALO, CONV_CH), lambda r: (nxt(r), ca)),
                  pl.BlockSpec((CONV_HALO, CONV_CH), lambda r: (nxt(r), cg)),
                  pl.BlockSpec((None, CONV_K, CONV_CH), lambda r: (l, 0, 0)),
                  vec(), vec(), vec()],
        out_specs=pl.BlockSpec((BLK, CONV_CH), lambda r: (r, 0)),
        out_shape=jax.ShapeDtypeStruct((T, CONV_CH), BF16),
        scratch_shapes=[pltpu.VMEM((BLK + 2 * CONV_HALO, CONV_CH), F32),
                        pltpu.VMEM((SUB, BLK + 3 * SUB, CONV_CH), F32)],
        compiler_params=_cp("parallel"),
        name="conv",
    )(z, z, z, z, z, z, conv_w, conv_b.reshape(DEPTH, 1, CONV_CH),
      ln_g.reshape(DEPTH, 1, CONV_CH), ln_b.reshape(DEPTH, 1, CONV_CH))


def _norm_router_kernel(x_ref, g_ref, mod_ref, rw_ref, rb_ref, h_ref, idx_ref, gate_ref):
    y = _rms(x_ref[...], g_ref[...])
    h = y * (1.0 + mod_ref[4:5, :]) + mod_ref[3:4, :]
    h_ref[...] = h
    hb = h.astype(BF16)
    h_lo = (h - hb.astype(F32)).astype(BF16)
    w = rw_ref[...]
    wb = w.astype(BF16)
    w_lo = (w - wb.astype(F32)).astype(BF16)
    dot = lambda a, b: jnp.dot(a, b, preferred_element_type=F32)
    logits = dot(hb, wb) + (dot(hb, w_lo) + dot(h_lo, wb)) + rb_ref[...]
    lane = lax.broadcasted_iota(jnp.int32, logits.shape, 1)
    logits = jnp.where(lane < N_EXP, logits, -jnp.inf)
    col = lax.broadcasted_iota(jnp.int32, idx_ref.shape, 1)
    idx_out = jnp.zeros(idx_ref.shape, jnp.int32)
    val_out = jnp.zeros(idx_ref.shape, F32)
    cur = logits
    for k in range(TOP_K):
        mx = jnp.max(cur, axis=-1, keepdims=True)
        ix = jnp.min(jnp.where(cur == mx, lane, N_EXP), axis=-1, keepdims=True)
        idx_out = jnp.where(col == k, ix, idx_out)
        val_out = jnp.where(col == k, mx, val_out)
        cur = jnp.where(lane == ix, -jnp.inf, cur)
    gate_ref[...] = _softmax(val_out)
    idx_ref[...] = idx_out


def norm_router(x, g, mod, router_w, router_b, l):
    tm = 512
    sd = jax.ShapeDtypeStruct
    return pl.pallas_call(
        _norm_router_kernel,
        grid=(T // tm,),
        in_specs=[pl.BlockSpec((tm, D), lambda m: (m, 0)),
                  pl.BlockSpec((None, 1, D), lambda m: (l, 0, 0)),
                  pl.BlockSpec((None, None, 6, D), lambda m: (l, _grp_of_row(m * tm), 0, 0)),
                  pl.BlockSpec((None, D, LANE), lambda m: (l, 0, 0)),
                  pl.BlockSpec((None, 1, LANE), lambda m: (l, 0, 0))],
        out_specs=[pl.BlockSpec((tm, D), lambda m: (m, 0)),
                   pl.BlockSpec((tm, TOP_K), lambda m: (m, 0)),
                   pl.BlockSpec((tm, TOP_K), lambda m: (m, 0))],
        out_shape=[sd((T, D), F32), sd((T, TOP_K), jnp.int32), sd((T, TOP_K), F32)],
        compiler_params=_cp("parallel"),
        name="norm_router",
    )(x, g.reshape(DEPTH, 1, D), mod,
      jnp.pad(router_w, ((0, 0), (0, 0), (0, LANE - N_EXP))),
      jnp.pad(router_b.reshape(DEPTH, 1, N_EXP), ((0, 0), (0, 0), (0, LANE - N_EXP))))


def _moe_kernel(be_ref, cnt_ref, nu_ref, x_ref, wg_ref, wu_ref, bg_ref, bu_ref, wo_ref, bo_ref,
                o_ref, xb_ref, act_ref, wgb_ref, wub_ref, wob_ref):
    del be_ref
    m = pl.program_id(0)
    j = pl.program_id(1)
    dot = lambda a, b: jnp.dot(a, b, preferred_element_type=F32)

    @pl.when(m < nu_ref[0])
    def _():
        nsub = (cnt_ref[m] + (MOE_SUB - 1)) // MOE_SUB

        @pl.when(j == 0)
        def _():
            def cast(s, _):
                r0 = pl.multiple_of(s * MOE_SUB, MOE_SUB)
                xb_ref[pl.ds(r0, MOE_SUB), :] = x_ref[pl.ds(r0, MOE_SUB), :].astype(BF16)
                return 0

            lax.fori_loop(0, nsub, cast, 0)

        @pl.when(j < MOE_NF)
        def _():
            wgb_ref[...] = wg_ref[...].astype(BF16)
            wub_ref[...] = wu_ref[...].astype(BF16)

            def sub(s, _):
                r0 = pl.multiple_of(s * MOE_SUB, MOE_SUB)
                x = xb_ref[pl.ds(r0, MOE_SUB), :]
                g = jnp.minimum(dot(x, wgb_ref[...]) + bg_ref[...], LIMIT)
                up = jnp.clip(dot(x, wub_ref[...]) + bu_ref[...], -LIMIT, LIMIT)
                act = (up + 1.0) * (g * jax.nn.sigmoid(ALPHA * g))
                act_ref[jnp.minimum(j, MOE_NF - 1), pl.ds(r0, MOE_SUB), :] = act.astype(BF16)
                return 0

            lax.fori_loop(0, nsub, sub, 0)

        @pl.when(j >= MOE_NF)
        def _():
            wob_ref[...] = wo_ref[...].astype(BF16)

            def sub(s, _):
                r0 = pl.multiple_of(s * MOE_SUB, MOE_SUB)
                y = bo_ref[...] + dot(act_ref[0, pl.ds(r0, MOE_SUB), :], wob_ref[0:MOE_TF, :])
                for f in range(1, MOE_NF):
                    y = y + dot(act_ref[f, pl.ds(r0, MOE_SUB), :],
                                wob_ref[f * MOE_TF:(f + 1) * MOE_TF, :])
                o_ref[pl.ds(r0, MOE_SUB), :] = y
                return 0

            lax.fori_loop(0, nsub, sub, 0)

            def zero(s, _):
                r0 = pl.multiple_of(s * MOE_SUB, MOE_SUB)
                o_ref[pl.ds(r0, MOE_SUB), :] = jnp.zeros((MOE_SUB, MOE_TN), F32)
                return 0

            lax.fori_loop(nsub, MOE_BM // MOE_SUB, zero, 0)


def moe_experts(xs, block_e, block_cnt, n_used, w_in, b_in, w_out, b_out, l):
    def mm(m, nu):
        return jnp.minimum(m, nu[0] - 1)

    def f1(m, j, nu):
        return jnp.where(m < nu[0], jnp.minimum(j, MOE_NF - 1), MOE_NF - 1)

    def n1(m, j, nu):
        return jnp.where(m < nu[0], jnp.clip(j - MOE_NF, 0, MOE_NN - 1), MOE_NN - 1)

    grid_spec = pltpu.PrefetchScalarGridSpec(
        num_scalar_prefetch=3,
        grid=(MOE_BLOCKS, MOE_NF + MOE_NN),
        in_specs=[
            pl.BlockSpec((MOE_BM, D), lambda m, j, be, cnt, nu: (mm(m, nu), 0)),
            pl.BlockSpec((None, None, D, MOE_TF),
                         lambda m, j, be, cnt, nu: (l, be[m], 0, f1(m, j, nu))),
            pl.BlockSpec((None, None, D, MOE_TF),
                         lambda m, j, be, cnt, nu: (l, be[m], 0, MOE_NF + f1(m, j, nu))),
            pl.BlockSpec((None, None, 1, MOE_TF),
                         lambda m, j, be, cnt, nu: (l, be[m], 0, f1(m, j, nu))),
            pl.BlockSpec((None, None, 1, MOE_TF),
                         lambda m, j, be, cnt, nu: (l, be[m], 0, MOE_NF + f1(m, j, nu))),
            pl.BlockSpec((None, None, FF, MOE_TN),
                         lambda m, j, be, cnt, nu: (l, be[m], 0, n1(m, j, nu))),
            pl.BlockSpec((None, None, 1, MOE_TN),
                         lambda m, j, be, cnt, nu: (l, be[m], 0, n1(m, j, nu))),
        ],
        out_specs=pl.BlockSpec((MOE_BM, MOE_TN), lambda m, j, be, cnt, nu: (mm(m, nu), n1(m, j, nu))),
        scratch_shapes=[pltpu.VMEM((MOE_BM, D), BF16),
                        pltpu.VMEM((MOE_NF, MOE_BM, MOE_TF), BF16),
                        pltpu.VMEM((D, MOE_TF), BF16), pltpu.VMEM((D, MOE_TF), BF16),
                        pltpu.VMEM((FF, MOE_TN), BF16)],
    )
    return pl.pallas_call(
        _moe_kernel,
        grid_spec=grid_spec,
        out_shape=jax.ShapeDtypeStruct((MOE_ROWS, D), F32),
        compiler_params=_cp("arbitrary", "arbitrary", vmem=60 << 20),
        name="moe_experts",
    )(block_e, block_cnt, n_used, xs, w_in, w_in,
      b_in.reshape(DEPTH, N_EXP, 1, 2 * FF), b_in.reshape(DEPTH, N_EXP, 1, 2 * FF),
      w_out, b_out.reshape(DEPTH, N_EXP, 1, D))


def _combine_kernel(x_ref, y_ref, gate_ref, mod_ref, o_ref):
    gate = gate_ref[...]
    acc = gate[:, 0:1] * y_ref[:, 0:D]
    for k in range(1, TOP_K):
        acc = acc + gate[:, k:k + 1] * y_ref[:, k * D:(k + 1) * D]
    o_ref[...] = x_ref[...] + mod_ref[5:6, :] * acc


def moe_combine(x, yg, gate, mod, l):
    tm = 256
    return pl.pallas_call(
        _combine_kernel,
        grid=(T // tm,),
        in_specs=[pl.BlockSpec((tm, D), lambda m: (m, 0)),
                  pl.BlockSpec((tm, TOP_K * D), lambda m: (m, 0)),
                  pl.BlockSpec((tm, TOP_K), lambda m: (m, 0)),
                  pl.BlockSpec((None, None, 6, D), lambda m: (l, _grp_of_row(m * tm), 0, 0))],
        out_specs=pl.BlockSpec((tm, D), lambda m: (m, 0)),
        out_shape=jax.ShapeDtypeStruct((T, D), F32),
        compiler_params=_cp("parallel"),
        name="moe_combine",
    )(x, yg, gate, mod)


def moe_routing(top_idx):
    n_assign = T * TOP_K
    flat_e = top_idx.reshape(-1)
    onehot = (flat_e[:, None] == jnp.arange(N_EXP, dtype=jnp.int32)[None, :]).astype(jnp.int32)
    csum = jnp.cumsum(onehot, axis=0)
    rank = jnp.take_along_axis(csum, flat_e[:, None], axis=1)[:, 0] - 1
    counts = csum[-1]
    nblk = (counts + MOE_BM - 1) // MOE_BM
    blk_end = jnp.cumsum(nblk)
    blk_start = blk_end - nblk
    n_used = blk_end[-1]
    dest = blk_start[flat_e] * MOE_BM + rank
    row_tok = jnp.zeros((MOE_ROWS,), jnp.int32).at[dest].set(
        jnp.arange(n_assign, dtype=jnp.int32) // TOP_K)
    mids = jnp.minimum(jnp.arange(MOE_BLOCKS, dtype=jnp.int32), n_used - 1)
    block_e = jnp.minimum(jnp.searchsorted(blk_end, mids, side='right'), N_EXP - 1).astype(jnp.int32)
    block_cnt = jnp.clip(counts[block_e] - (mids - blk_start[block_e]) * MOE_BM, 0, MOE_BM)
    block_cnt = jnp.where(jnp.arange(MOE_BLOCKS) < n_used, block_cnt, 0).astype(jnp.int32)
    return row_tok, dest, block_e, block_cnt, n_used.reshape(1).astype(jnp.int32)


def trunk_layer(x, l, mod, p, ssm_p, rope, cache_k, cache_v, h0):
    lambda_init = 0.8 - 0.6 * math.exp(-0.3 * l)
    zero_b = lambda n: jnp.zeros((DEPTH, 1, n), F32)

    h = norm_mod(x, p['norm_mix'], mod, l)
    z = matmul(h, p['w_in'], zero_b(IN_W), l, name="w_in")

    att = attn_ctx(z, p['diff_lambda'], p['diff_subln'], l, lambda_init)
    att = attn_lat(z, cache_k, cache_v, rope[0], rope[1], p['diff_lambda'], p['diff_subln'],
                   att, l, lambda_init)
    att_out = matmul(att, p['w_attn_out'], zero_b(D), l, name="attn_out")

    ab, bd, cd = ssm_p
    u_tm = jnp.transpose(z.reshape(N_BLK, BLK, IN_W)[:, :, 3 * QKV_W:3 * QKV_W + SSM_W], (1, 0, 2))
    y_tm, fin = ssm_mixer(u_tm, bd, cd, ab, p['ssm_d'].reshape(DEPTH, N_CC, 1, U_CH), h0, l)
    y = jnp.transpose(y_tm, (1, 0, 2)).reshape(T, SSM_W).astype(BF16)
    ssm_out = matmul_glu(y, p['w_ssm_glu'], l)

    cnv = conv_module(z, p['conv_w'], p['conv_b'], p['conv_ln_g'], p['conv_ln_b'], l)
    conv_out = matmul(cnv, p['w_conv_out'], p['b_conv_out'].reshape(DEPTH, 1, D), l, name="conv_out")

    mixed = merge_gates(h, p['w_merge_gate'], p['b_merge_gate'].reshape(DEPTH, 1, 3 * D),
                        att_out, ssm_out, conv_out, l)
    x = matmul_resid(mixed, p['w_out'], x, mod, l, gate_row=2)

    h2p, top_idx, gate = norm_router(x, p['norm_ffn'], mod, p['router_w'], p['router_b'], l)
    row_tok, dest, block_e, block_cnt, n_used = moe_routing(top_idx)
    xs = h2p.at[row_tok].get(mode='promise_in_bounds')
    ys = moe_experts(xs, block_e, block_cnt, n_used,
                     p['moe_w_in'], p['moe_b_in'], p['moe_w_out'], p['moe_b_out'], l)
    yg = ys.at[dest].get(mode='promise_in_bounds').reshape(T, TOP_K * D)
    x = moe_combine(x, yg, gate, mod, l)

    k_new = z[:T_CTX, QKV_W:2 * QKV_W].reshape(BATCH, SEQ, 2 * HEADS, HEAD_DIM)
    v_new = z[:T_CTX, 2 * QKV_W:3 * QKV_W].reshape(BATCH, SEQ, HEADS, 2 * HEAD_DIM)
    s_new = jnp.transpose(fin[:, :, :BATCH, :], (2, 0, 1, 3)).reshape(BATCH, 2, 2, SSM_G, SSM_N)
    return x, k_new, v_new, s_new


def kernel(x_prompt, x_sample, cache_k, cache_v, state_ssm, c, c_ctx, w_ada, b_ada, norm_mix, norm_ffn, w_in, diff_lambda, diff_subln, w_attn_out, ssm_a_re, ssm_a_im, ssm_log_dt, ssm_b, ssm_c, ssm_d, w_ssm_glu, conv_w, conv_b, conv_ln_g, conv_ln_b, w_conv_out, b_conv_out, w_merge_gate, b_merge_gate, w_out, router_w, router_b, moe_w_in, moe_b_in, moe_w_out, moe_b_out, norm_final):
    p = dict(norm_mix=norm_mix, norm_ffn=norm_ffn, w_in=w_in, diff_lambda=diff_lambda,
             diff_subln=diff_subln, w_attn_out=w_attn_out, ssm_d=ssm_d, w_ssm_glu=w_ssm_glu,
             conv_w=conv_w, conv_b=conv_b, conv_ln_g=conv_ln_g, conv_ln_b=conv_ln_b,
             w_conv_out=w_conv_out, b_conv_out=b_conv_out, w_merge_gate=w_merge_gate,
             b_merge_gate=b_merge_gate, w_out=w_out, router_w=router_w, router_b=router_b,
             moe_w_in=moe_w_in, moe_b_in=moe_b_in, moe_w_out=moe_w_out, moe_b_out=moe_b_out)

    cond8 = jnp.zeros((SUB, D), F32).at[0].set(c_ctx).at[1:1 + DEC_BATCH].set(c)
    mod = ada_all(cond8, w_ada, b_ada).reshape(DEPTH, SUB, 6, D)

    ab_re, ab_im, bb_re, bb_im = ssm_discretize(ssm_a_re, ssm_a_im, ssm_log_dt, ssm_b)
    ssm_p = ssm_weights(ab_re, ab_im, bb_re, bb_im, ssm_c)
    rope = rope_tables()

    ck = cache_k.reshape(DEC_BATCH, DEPTH, PAST, QKV_W)
    cv = cache_v.reshape(DEC_BATCH, DEPTH, PAST, QKV_W)
    st = state_ssm.reshape(DEC_BATCH, DEPTH, 2, 2, 1, SSM_G * SSM_N)
    h0_all = jnp.concatenate([jnp.zeros((N_SG - DEC_BATCH,) + st.shape[1:], F32), st], axis=0)

    x = jnp.concatenate([x_prompt.reshape(T_CTX, D), x_sample.reshape(T_LAT, D)], axis=0)
    new_k, new_v, new_s = [], [], []
    for l in range(DEPTH):
        x, k_l, v_l, s_l = trunk_layer(x, l, mod, p, ssm_p, rope, ck, cv, h0_all[:, l])
        new_k.append(k_l)
        new_v.append(v_l)
        new_s.append(s_l)
    y = final_norm(x, norm_final)
    return (y[:T_CTX].reshape(BATCH, SEQ, D), y[T_CTX:].reshape(DEC_BATCH, DEC_SEQ, D),
            jnp.stack(new_k, axis=1), jnp.stack(new_v, axis=1), jnp.stack(new_s, axis=1))
```

```python
import functools
import math

import jax
import jax.numpy as jnp
from jax import lax
from jax.experimental import pallas as pl
from jax.experimental.pallas import tpu as pltpu

F32 = jnp.float32
BF16 = jnp.bfloat16

D = 2048
BATCH = 32
SEQ = 256
DEPTH = 4
DEC_BATCH = 2
DEC_SEQ = 2048
PAST = 512
GRID_W = 64
HEADS = 8
HEAD_DIM = 64
ROPE_PAIRS = 16
ROPE_BASE = 10000.0
QKV_W = 1024
SSM_W = 1024
SSM_G = 64
SSM_P = 16
SSM_N = 64
CONV_CH = 1024
CONV_K = 31
IN_W = 6144
N_EXP = 32
TOP_K = 4
FF = 2048
LIMIT = 7.0
ALPHA = 1.702
EPS = 1e-6

T_CTX = BATCH * SEQ
T_LAT = DEC_BATCH * DEC_SEQ
T = T_CTX + T_LAT
BLK = 256
N_BLK = T // BLK
CTX_BLKS = T_CTX // BLK
LAT_BLKS = DEC_SEQ // BLK
N_GRP = 1 + DEC_BATCH
SUB = 8
LANE = 128
STATE_CH = 1024
U_CH = 256
N_CC = SSM_W // U_CH
N_SG = N_BLK // SUB

MOE_BM = 1024
MOE_SUB = 256
MOE_TF = 512
MOE_NF = FF // MOE_TF
MOE_TN = 256
MOE_NN = D // MOE_TN
MOE_BLOCKS = (T * TOP_K) // MOE_BM + N_EXP
MOE_ROWS = MOE_BLOCKS * MOE_BM

VMEM_LIMIT = 56 << 20


def _cp(*sem, vmem=VMEM_LIMIT):
    return pltpu.CompilerParams(dimension_semantics=sem, vmem_limit_bytes=vmem)


def _grp_of_row(row):
    return jnp.where(row < T_CTX, 0, 1 + (row - T_CTX) // DEC_SEQ)


def _ada_kernel(c_ref, w_ref, b_ref, o_ref):
    c = c_ref[...]
    a = (c * jax.nn.sigmoid(c)).astype(BF16)
    o_ref[...] = jnp.dot(a, w_ref[...].astype(BF16), preferred_element_type=F32) + b_ref[...]


def ada_all(cond8, w_ada, b_ada):
    tn = 1024
    n = 6 * D
    return pl.pallas_call(
        _ada_kernel,
        grid=(DEPTH, n // tn),
        in_specs=[pl.BlockSpec((SUB, D), lambda l, j: (0, 0)),
                  pl.BlockSpec((None, D, tn), lambda l, j: (l, 0, j)),
                  pl.BlockSpec((None, 1, tn), lambda l, j: (l, 0, j))],
        out_specs=pl.BlockSpec((None, SUB, tn), lambda l, j: (l, 0, j)),
        out_shape=jax.ShapeDtypeStruct((DEPTH, SUB, n), F32),
        compiler_params=_cp("parallel", "parallel"),
        name="ada",
    )(cond8, w_ada, b_ada.reshape(DEPTH, 1, n))


def _rms(x, g):
    return x * lax.rsqrt(jnp.mean(x * x, axis=-1, keepdims=True) + EPS) * g


def _norm_mod_kernel(x_ref, g_ref, mod_ref, o_ref):
    y = _rms(x_ref[...], g_ref[...])
    sh = mod_ref[0:1, :]
    sc = mod_ref[1:2, :]
    o_ref[...] = (y * (1.0 + sc) + sh).astype(o_ref.dtype)


def norm_mod(x, g, mod, l):
    tm = 512
    return pl.pallas_call(
        _norm_mod_kernel,
        grid=(T // tm,),
        in_specs=[pl.BlockSpec((tm, D), lambda m: (m, 0)),
                  pl.BlockSpec((None, 1, D), lambda m: (l, 0, 0)),
                  pl.BlockSpec((None, None, 6, D), lambda m: (l, _grp_of_row(m * tm), 0, 0))],
        out_specs=pl.BlockSpec((tm, D), lambda m: (m, 0)),
        out_shape=jax.ShapeDtypeStruct((T, D), BF16),
        compiler_params=_cp("parallel"),
        name="norm_mod",
    )(x, g.reshape(DEPTH, 1, D), mod)


def _final_norm_kernel(x_ref, g_ref, o_ref):
    o_ref[...] = _rms(x_ref[...], g_ref[...])


def final_norm(x, g):
    tm = 512
    return pl.pallas_call(
        _final_norm_kernel,
        grid=(T // tm,),
        in_specs=[pl.BlockSpec((tm, D), lambda m: (m, 0)),
                  pl.BlockSpec((1, D), lambda m: (0, 0))],
        out_specs=pl.BlockSpec((tm, D), lambda m: (m, 0)),
        out_shape=jax.ShapeDtypeStruct((T, D), F32),
        compiler_params=_cp("parallel"),
        name="final_norm",
    )(x, g.reshape(1, D))


def _mm_kernel(a_ref, w_ref, b_ref, o_ref, wb_ref):
    @pl.when(pl.program_id(1) == 0)
    def _():
        wb_ref[...] = w_ref[...].astype(BF16)

    acc = jnp.dot(a_ref[...], wb_ref[...], preferred_element_type=F32)
    o_ref[...] = (acc + b_ref[...]).astype(o_ref.dtype)


def matmul(a, w, bias, l, *, tm=1024, tn=1024, out_dtype=F32, name="mm"):
    k, n = w.shape[1], w.shape[2]
    return pl.pallas_call(
        _mm_kernel,
        grid=(n // tn, T // tm),
        in_specs=[pl.BlockSpec((tm, k), lambda j, m: (m, 0)),
                  pl.BlockSpec((None, k, tn), lambda j, m: (l, 0, j)),
                  pl.BlockSpec((None, 1, tn), lambda j, m: (l, 0, j))],
        out_specs=pl.BlockSpec((tm, tn), lambda j, m: (m, j)),
        out_shape=jax.ShapeDtypeStruct((T, n), out_dtype),
        scratch_shapes=[pltpu.VMEM((k, tn), BF16)],
        compiler_params=_cp("parallel", "arbitrary"),
        name=name,
    )(a, w, bias)


def _mm_glu_kernel(a_ref, wa_ref, wg_ref, o_ref, wab_ref, wgb_ref):
    @pl.when(pl.program_id(1) == 0)
    def _():
        wab_ref[...] = wa_ref[...].astype(BF16)
        wgb_ref[...] = wg_ref[...].astype(BF16)

    a = a_ref[...]
    va = jnp.dot(a, wab_ref[...], preferred_element_type=F32)
    vg = jnp.dot(a, wgb_ref[...], preferred_element_type=F32)
    o_ref[...] = (va * jax.nn.sigmoid(vg)).astype(o_ref.dtype)


def matmul_glu(a, w, l, *, tm=1024, tn=512):
    k, n = w.shape[1], w.shape[2] // 2
    nj = n // tn
    return pl.pallas_call(
        _mm_glu_kernel,
        grid=(nj, T // tm),
        in_specs=[pl.BlockSpec((tm, k), lambda j, m: (m, 0)),
                  pl.BlockSpec((None, k, tn), lambda j, m: (l, 0, j)),
                  pl.BlockSpec((None, k, tn), lambda j, m: (l, 0, nj + j))],
        out_specs=pl.BlockSpec((tm, tn), lambda j, m: (m, j)),
        out_shape=jax.ShapeDtypeStruct((T, n), BF16),
        scratch_shapes=[pltpu.VMEM((k, tn), BF16), pltpu.VMEM((k, tn), BF16)],
        compiler_params=_cp("parallel", "arbitrary"),
        name="ssm_glu",
    )(a, w, w)


def _mm_resid_kernel(a_ref, w_ref, x_ref, mod_ref, o_ref, wb_ref, *, gate_row):
    @pl.when(pl.program_id(1) == 0)
    def _():
        wb_ref[...] = w_ref[...].astype(BF16)

    acc = jnp.dot(a_ref[...], wb_ref[...], preferred_element_type=F32)
    o_ref[...] = x_ref[...] + mod_ref[gate_row:gate_row + 1, :] * acc


def matmul_resid(a, w, x, mod, l, *, gate_row, tm=1024, tn=1024):
    k, n = w.shape[1], w.shape[2]
    return pl.pallas_call(
        functools.partial(_mm_resid_kernel, gate_row=gate_row),
        grid=(n // tn, T // tm),
        in_specs=[pl.BlockSpec((tm, k), lambda j, m: (m, 0)),
                  pl.BlockSpec((None, k, tn), lambda j, m: (l, 0, j)),
                  pl.BlockSpec((tm, tn), lambda j, m: (m, j)),
                  pl.BlockSpec((None, None, 6, tn), lambda j, m: (l, _grp_of_row(m * tm), 0, j))],
        out_specs=pl.BlockSpec((tm, tn), lambda j, m: (m, j)),
        out_shape=jax.ShapeDtypeStruct((T, n), F32),
        scratch_shapes=[pltpu.VMEM((k, tn), BF16)],
        compiler_params=_cp("parallel", "arbitrary"),
        name="out_resid",
    )(a, w, x, mod)


def _merge_kernel(h_ref, w0_ref, w1_ref, w2_ref, b0_ref, b1_ref, b2_ref,
                  att_ref, ssm_ref, cnv_ref, o_ref, wb0_ref, wb1_ref, wb2_ref):
    @pl.when(pl.program_id(1) == 0)
    def _():
        wb0_ref[...] = w0_ref[...].astype(BF16)
        wb1_ref[...] = w1_ref[...].astype(BF16)
        wb2_ref[...] = w2_ref[...].astype(BF16)

    h = h_ref[...]

    def gate(wb_ref, b_ref):
        return jax.nn.sigmoid(jnp.dot(h, wb_ref[...], preferred_element_type=F32) + b_ref[...])

    mixed = (gate(wb0_ref, b0_ref) * att_ref[...] + gate(wb1_ref, b1_ref) * ssm_ref[...]
             + gate(wb2_ref, b2_ref) * cnv_ref[...])
    o_ref[...] = mixed.astype(o_ref.dtype)


def merge_gates(h, w, b, att, ssm, cnv, l, *, tm=512, tn=512):
    nj = D // tn
    wspec = lambda br: pl.BlockSpec((None, D, tn), lambda j, m: (l, 0, br * nj + j))
    bspec = lambda br: pl.BlockSpec((None, 1, tn), lambda j, m: (l, 0, br * nj + j))
    tile = pl.BlockSpec((tm, tn), lambda j, m: (m, j))
    return pl.pallas_call(
        _merge_kernel,
        grid=(nj, T // tm),
        in_specs=[pl.BlockSpec((tm, D), lambda j, m: (m, 0)),
                  wspec(0), wspec(1), wspec(2), bspec(0), bspec(1), bspec(2),
                  tile, tile, tile],
        out_specs=tile,
        out_shape=jax.ShapeDtypeStruct((T, D), BF16),
        scratch_shapes=[pltpu.VMEM((D, tn), BF16)] * 3,
        compiler_params=_cp("parallel", "arbitrary"),
        name="merge_gates",
    )(h, w, w, w, b, b, b, att, ssm, cnv)


def _lambda(lam_ref, lambda_init):
    lp = lam_ref[...]
    s01 = jnp.sum(lp[0:1, :] * lp[1:2, :], axis=-1, keepdims=True)
    s23 = jnp.sum(lp[2:3, :] * lp[3:4, :], axis=-1, keepdims=True)
    return jnp.exp(s01) - jnp.exp(s23) + lambda_init


def _softmax(s):
    e = jnp.exp(s - jnp.max(s, axis=-1, keepdims=True))
    return e * (1.0 / jnp.sum(e, axis=-1, keepdims=True))


def _diff_attn_head(q, kb, vb, lam, g, lambda_init):
    lane = lax.broadcasted_iota(jnp.int32, q.shape, 1)
    q_a = jnp.where(lane < HEAD_DIM, q, 0.0).astype(BF16)
    q_b = jnp.where(lane >= HEAD_DIM, q, 0.0).astype(BF16)
    nt = (((1,), (1,)), ((), ()))
    p_a = _softmax(lax.dot_general(q_a, kb, nt, preferred_element_type=F32))
    p_b = _softmax(lax.dot_general(q_b, kb, nt, preferred_element_type=F32))
    a = (p_a - lam * p_b).astype(BF16)
    o = jnp.dot(a, vb, preferred_element_type=F32)
    return _rms(o, g) * (1.0 - lambda_init)


def _attn_ctx_kernel(q_ref, k_ref, v_ref, lam_ref, g_ref, o_ref, *, lambda_init):
    lam = _lambda(lam_ref, lambda_init)
    g = g_ref[...]
    for h in range(HEADS):
        cols = slice(h * LANE, (h + 1) * LANE)
        q = q_ref[:, cols] * (HEAD_DIM ** -0.5)
        o = _diff_attn_head(q, k_ref[:, cols].astype(BF16), v_ref[:, cols].astype(BF16),
                            lam, g, lambda_init)
        o_ref[:, cols] = o.astype(o_ref.dtype)


def attn_ctx(z, lam_p, subln, l, lambda_init):
    return pl.pallas_call(
        functools.partial(_attn_ctx_kernel, lambda_init=lambda_init),
        grid=(BATCH,),
        in_specs=[pl.BlockSpec((SEQ, QKV_W), lambda b: (b, 0)),
                  pl.BlockSpec((SEQ, QKV_W), lambda b: (b, 1)),
                  pl.BlockSpec((SEQ, QKV_W), lambda b: (b, 2)),
                  pl.BlockSpec((None, 4, HEAD_DIM), lambda b: (l, 0, 0)),
                  pl.BlockSpec((None, 1, LANE), lambda b: (l, 0, 0))],
        out_specs=pl.BlockSpec((SEQ, QKV_W), lambda b: (b, 0)),
        out_shape=jax.ShapeDtypeStruct((T_CTX, QKV_W), BF16),
        compiler_params=_cp("parallel"),
        name="attn_ctx",
    )(z, z, z, lam_p, subln.reshape(DEPTH, 1, LANE))


def _rope(x, cos, sin_signed):
    lane = lax.broadcasted_iota(jnp.int32, x.shape, 1)
    first = (lane % (2 * ROPE_PAIRS)) < ROPE_PAIRS
    partner = jnp.where(first, pltpu.roll(x, LANE - ROPE_PAIRS, 1), pltpu.roll(x, ROPE_PAIRS, 1))
    return x * cos + partner * sin_signed


def _attn_lat_kernel(q_ref, k_ref, v_ref, ck_ref, cv_ref, cq_ref, sq_ref, ckk_ref, skk_ref,
                     lam_ref, g_ref, o_ref, kb_ref, vb_ref, *, lambda_init):
    @pl.when(pl.program_id(2) == 0)
    def _():
        kb_ref[0:DEC_SEQ, :] = _rope(k_ref[...], ckk_ref[...], skk_ref[...]).astype(BF16)
        kb_ref[DEC_SEQ:DEC_SEQ + PAST, :] = ck_ref[...].astype(BF16)
        vb_ref[0:DEC_SEQ, :] = v_ref[...].astype(BF16)
        vb_ref[DEC_SEQ:DEC_SEQ + PAST, :] = cv_ref[...].astype(BF16)

    q = _rope(q_ref[...], cq_ref[...], sq_ref[...]) * (HEAD_DIM ** -0.5)
    o = _diff_attn_head(q, kb_ref[...], vb_ref[...], _lambda(lam_ref, lambda_init), g_ref[...],
                        lambda_init)
    o_ref[...] = o.astype(o_ref.dtype)


def attn_lat(z, cache_k, cache_v, cos_t, sin_t, lam_p, subln, l, lambda_init):
    tq = BLK
    nq = DEC_SEQ // tq
    qrow = lambda b, h, i: CTX_BLKS + b * nq + i
    krow = lambda b: T_CTX // DEC_SEQ + b
    return pl.pallas_call(
        functools.partial(_attn_lat_kernel, lambda_init=lambda_init),
        grid=(DEC_BATCH, HEADS, nq),
        in_specs=[pl.BlockSpec((tq, LANE), lambda b, h, i: (qrow(b, h, i), h)),
                  pl.BlockSpec((DEC_SEQ, LANE), lambda b, h, i: (krow(b), HEADS + h)),
                  pl.BlockSpec((DEC_SEQ, LANE), lambda b, h, i: (krow(b), 2 * HEADS + h)),
                  pl.BlockSpec((None, None, PAST, LANE), lambda b, h, i: (b, l, 0, h)),
                  pl.BlockSpec((None, None, PAST, LANE), lambda b, h, i: (b, l, 0, h)),
                  pl.BlockSpec((tq, LANE), lambda b, h, i: (i, 0)),
                  pl.BlockSpec((tq, LANE), lambda b, h, i: (i, 0)),
                  pl.BlockSpec((DEC_SEQ, LANE), lambda b, h, i: (0, 0)),
                  pl.BlockSpec((DEC_SEQ, LANE), lambda b, h, i: (0, 0)),
                  pl.BlockSpec((None, 4, HEAD_DIM), lambda b, h, i: (l, 0, 0)),
                  pl.BlockSpec((None, 1, LANE), lambda b, h, i: (l, 0, 0))],
        out_specs=pl.BlockSpec((tq, LANE), lambda b, h, i: (b * nq + i, h)),
        out_shape=jax.ShapeDtypeStruct((T_LAT, QKV_W), BF16),
        scratch_shapes=[pltpu.VMEM((DEC_SEQ + PAST, LANE), BF16)] * 2,
        compiler_params=_cp("parallel", "parallel", "arbitrary"),
        name="attn_lat",
    )(z, z, z, cache_k, cache_v, cos_t, sin_t, cos_t, sin_t, lam_p,
      subln.reshape(DEPTH, 1, LANE))


def rope_tables():
    row = (jnp.arange(DEC_SEQ) // GRID_W).astype(F32)
    col = (jnp.arange(DEC_SEQ) % GRID_W).astype(F32)
    inv_freq = ROPE_BASE ** (-jnp.arange(ROPE_PAIRS, dtype=F32) / ROPE_PAIRS)
    ang = jnp.stack([row[:, None] * inv_freq, col[:, None] * inv_freq], axis=1)
    cos, sin = jnp.cos(ang), jnp.sin(ang)
    cos64 = jnp.concatenate([cos, cos], axis=-1).reshape(DEC_SEQ, HEAD_DIM)
    sin64 = jnp.concatenate([-sin, sin], axis=-1).reshape(DEC_SEQ, HEAD_DIM)
    return jnp.tile(cos64, (1, 2)), jnp.tile(sin64, (1, 2))


def _zoh_kernel(are_ref, aim_ref, ldt_ref, bre_ref, bim_ref,
                abr_ref, abi_ref, bbr_ref, bbi_ref):
    a_re = are_ref[...]
    a_im = aim_ref[...]
    dt = jnp.exp(ldt_ref[...])
    mag = jnp.exp(dt * a_re)
    ang = dt * a_im
    ab_re = mag * jnp.cos(ang)
    ab_im = mag * jnp.sin(ang)
    den = a_re * a_re + a_im * a_im
    xm = ab_re - 1.0
    f_re = (xm * a_re + ab_im * a_im) / den
    f_im = (ab_im * a_re - xm * a_im) / den
    abr_ref[...] = ab_re
    abi_ref[...] = ab_im
    rows = a_re.shape[0]
    fr = jnp.broadcast_to(f_re[:, None, :], (rows, SSM_P, SSM_N))
    fi = jnp.broadcast_to(f_im[:, None, :], (rows, SSM_P, SSM_N))
    b_re = bre_ref[...]
    b_im = bim_ref[...]
    bbr_ref[...] = fr * b_re - fi * b_im
    bbi_ref[...] = fr * b_im + fi * b_re


def ssm_discretize(a_re, a_im, log_dt, b):
    rows = DEPTH * 2 * SSM_G
    bt = jnp.swapaxes(b, -1, -2)
    b_re = bt[:, :, 0].reshape(rows, SSM_P, SSM_N)
    b_im = bt[:, :, 1].reshape(rows, SSM_P, SSM_N)
    sd = jax.ShapeDtypeStruct
    ab_re, ab_im, bb_re, bb_im = pl.pallas_call(
        _zoh_kernel,
        out_shape=[sd((rows, SSM_N), F32), sd((rows, SSM_N), F32),
                   sd((rows, SSM_P, SSM_N), F32), sd((rows, SSM_P, SSM_N), F32)],
        compiler_params=pltpu.CompilerParams(vmem_limit_bytes=VMEM_LIMIT),
        name="ssm_zoh",
    )(a_re.reshape(rows, SSM_N), a_im.reshape(rows, SSM_N), log_dt.reshape(rows, 1), b_re, b_im)
    return ab_re, ab_im, bb_re, bb_im


def _block_diag(m):
    eye = jnp.eye(16, dtype=m.dtype)
    out = m[..., :, :, None, :] * eye[:, None, :, None]
    return out.reshape(*m.shape[:-3], 16 * m.shape[-2], 16 * m.shape[-1])


def ssm_weights(ab_re, ab_im, bb_re, bb_im, c):
    ab = jnp.stack([ab_re, ab_im], axis=0).reshape(2, DEPTH, 2, N_CC, 1, STATE_CH)
    ab = jnp.transpose(ab, (1, 2, 0, 3, 4, 5))
    bb = jnp.stack([bb_re, bb_im], axis=0).reshape(2, DEPTH, 2, N_CC, 16, SSM_P, SSM_N)
    bd = _block_diag(jnp.transpose(bb, (1, 2, 0, 3, 4, 5, 6))).astype(BF16)
    ct = jnp.swapaxes(c, -1, -2).reshape(DEPTH, 2, 2, N_CC, 16, SSM_N, SSM_P)
    cd = _block_diag(ct).astype(BF16)
    return ab, bd, cd


def _cmul_add(ar, ai, xr, xi, br, bi):
    return ar * xr - ai * xi + br, ar * xi + ai * xr + bi


def _ssm_kernel(u_ref, bd_ref, cd_ref, ab_ref, d_ref, h0_ref, y_ref, fin_ref,
                xr_ref, xi_ref, tr_ref, ti_ref, acc_ref, cin_ref):
    nj = BLK
    is_lat = pl.program_id(0) >= CTX_BLKS // SUB
    u = u_ref[...].reshape(nj * SUB, U_CH)
    ub = u.astype(BF16)
    acc_ref[...] = u * d_ref[...]
    for d in range(2):
        xr_ref[...] = jnp.dot(ub, bd_ref[d, 0], preferred_element_type=F32)
        xi_ref[...] = jnp.dot(ub, bd_ref[d, 1], preferred_element_type=F32)
        a1r = ab_ref[d, 0]
        a1i = ab_ref[d, 1]
        ar = jnp.broadcast_to(a1r, (SUB, STATE_CH))
        ai = jnp.broadcast_to(a1i, (SUB, STATE_CH))

        def step(i, carry, d=d, ar=ar, ai=ai):
            j = i if d == 0 else nj - 1 - i
            r0 = pl.multiple_of(j * SUB, SUB)
            nr, ni = _cmul_add(ar, ai, carry[0], carry[1],
                               xr_ref[pl.ds(r0, SUB), :], xi_ref[pl.ds(r0, SUB), :])
            xr_ref[pl.ds(r0, SUB), :] = nr
            xi_ref[pl.ds(r0, SUB), :] = ni
            return nr, ni

        zero = jnp.zeros((SUB, STATE_CH), F32)
        lax.fori_loop(0, nj, step, (zero, zero), unroll=4)

        last = (nj - 1) * SUB if d == 0 else 0
        fin_ref[d, 0] = xr_ref[last:last + SUB, :]
        fin_ref[d, 1] = xi_ref[last:last + SUB, :]

        @pl.when(is_lat)
        def _(d=d, a1r=a1r, a1i=a1i, last=last):
            def pstep(j, p):
                tr_ref[pl.ds(j, 1), :] = p[0]
                ti_ref[pl.ds(j, 1), :] = p[1]
                return _cmul_add(a1r, a1i, p[0], p[1], 0.0, 0.0)

            lax.fori_loop(0, nj, pstep, (a1r, a1i))
            anr = tr_ref[nj - 1:nj, :]
            ani = ti_ref[nj - 1:nj, :]
            cr = [None] * SUB
            ci = [None] * SUB
            order = list(range(SUB)) if d == 0 else list(range(SUB - 1, -1, -1))
            cr[order[0]] = h0_ref[d, 0]
            ci[order[0]] = h0_ref[d, 1]
            for prev, cur in zip(order[:-1], order[1:]):
                fr = xr_ref[last + prev:last + prev + 1, :]
                fi = xi_ref[last + prev:last + prev + 1, :]
                cr[cur], ci[cur] = _cmul_add(anr, ani, cr[prev], ci[prev], fr, fi)
            for s in range(SUB):
                cin_ref[0, s:s + 1, :] = cr[s]
                cin_ref[1, s:s + 1, :] = ci[s]
            cin_r = cin_ref[0]
            cin_i = cin_ref[1]

            def fstep(i, _):
                t = i if d == 0 else nj - 1 - i
                r0 = pl.multiple_of(i * SUB, SUB)
                pr = tr_ref[pl.ds(t, 1), :]
                pi = ti_ref[pl.ds(t, 1), :]
                nr, ni = _cmul_add(pr, pi, cin_r, cin_i,
                                   xr_ref[pl.ds(r0, SUB), :], xi_ref[pl.ds(r0, SUB), :])
                xr_ref[pl.ds(r0, SUB), :] = nr
                xi_ref[pl.ds(r0, SUB), :] = ni
                return 0

            lax.fori_loop(0, nj, fstep, 0, unroll=2)

        acc_ref[...] += (jnp.dot(xr_ref[...].astype(BF16), cd_ref[d, 0], preferred_element_type=F32)
                         - jnp.dot(xi_ref[...].astype(BF16), cd_ref[d, 1], preferred_element_type=F32))

    y_ref[...] = jax.nn.gelu(acc_ref[...]).reshape(nj, SUB, U_CH)


def ssm_mixer(u_tm, bd, cd, ab, dvec, h0, l):
    nrow = BLK * SUB
    sd = jax.ShapeDtypeStruct
    return pl.pallas_call(
        _ssm_kernel,
        grid=(N_SG, N_CC),
        in_specs=[pl.BlockSpec((BLK, SUB, U_CH), lambda s, c: (0, s, c)),
                  pl.BlockSpec((None, 2, 2, None, U_CH, STATE_CH), lambda s, c: (l, 0, 0, c, 0, 0)),
                  pl.BlockSpec((None, 2, 2, None, STATE_CH, U_CH), lambda s, c: (l, 0, 0, c, 0, 0)),
                  pl.BlockSpec((None, 2, 2, None, 1, STATE_CH), lambda s, c: (l, 0, 0, c, 0, 0)),
                  pl.BlockSpec((None, None, 1, U_CH), lambda s, c: (l, c, 0, 0)),
                  pl.BlockSpec((None, 2, 2, 1, STATE_CH), lambda s, c: (s, 0, 0, 0, c))],
        out_specs=[pl.BlockSpec((BLK, SUB, U_CH), lambda s, c: (0, s, c)),
                   pl.BlockSpec((2, 2, SUB, STATE_CH), lambda s, c: (0, 0, s, c))],
        out_shape=[sd((BLK, N_BLK, SSM_W), F32), sd((2, 2, N_BLK, SSM_G * SSM_N), F32)],
        scratch_shapes=[pltpu.VMEM((nrow, STATE_CH), F32), pltpu.VMEM((nrow, STATE_CH), F32),
                        pltpu.VMEM((BLK, STATE_CH), F32), pltpu.VMEM((BLK, STATE_CH), F32),
                        pltpu.VMEM((nrow, U_CH), F32), pltpu.VMEM((2, SUB, STATE_CH), F32)],
        compiler_params=_cp("parallel", "parallel"),
        name="ssm",
    )(u_tm, bd, cd, ab, dvec, h0)


CONV_HALO = 16
CONV_RC = 32


def _conv_kernel(xa_ref, xg_ref, pa_ref, pg_ref, na_ref, ng_ref, w_ref, cb_ref, lg_ref, lb_ref,
                 o_ref, xp_ref, xs_ref):
    r = pl.program_id(0)
    is_lat = r >= CTX_BLKS
    pos = jnp.bitwise_and(r, LAT_BLKS - 1)
    pv = jnp.where(is_lat & (pos != 0), 1.0, 0.0)
    nv = jnp.where(is_lat & (pos != LAT_BLKS - 1), 1.0, 0.0)
    glu = lambda a_ref, g_ref: a_ref[...] * jax.nn.sigmoid(g_ref[...])
    xp_ref[0:CONV_HALO, :] = glu(pa_ref, pg_ref) * pv
    xp_ref[CONV_HALO:CONV_HALO + BLK, :] = glu(xa_ref, xg_ref)
    xp_ref[CONV_HALO + BLK:, :] = glu(na_ref, ng_ref) * nv
    span = BLK + 3 * SUB
    for s in range(SUB):
        xs_ref[s] = xp_ref[s:s + span, :]

    def chunk(c, _):
        r0 = pl.multiple_of(c * CONV_RC, CONV_RC)
        acc = jnp.zeros((CONV_RC, CONV_CH), F32)
        for k in range(CONV_K):
            off = k + CONV_HALO - CONV_K // 2
            acc = acc + xs_ref[off % SUB, pl.ds(r0 + (off // SUB) * SUB, CONV_RC), :] * w_ref[k:k + 1, :]
        acc = acc + cb_ref[...]
        mu = jnp.mean(acc, axis=-1, keepdims=True)
        xc = acc - mu
        var = jnp.mean(xc * xc, axis=-1, keepdims=True)
        yn = xc * lax.rsqrt(var + EPS) * lg_ref[...] + lb_ref[...]
        o_ref[pl.ds(r0, CONV_RC), :] = (yn * jax.nn.sigmoid(yn)).astype(o_ref.dtype)
        return 0

    lax.fori_loop(0, BLK // CONV_RC, chunk, 0)


def conv_module(z, conv_w, conv_b, ln_g, ln_b, l):
    hb = BLK // CONV_HALO
    last_halo = T // CONV_HALO - 1
    ca, cg = 4, 5
    prev = lambda r: jnp.maximum(r * hb - 1, 0)
    nxt = lambda r: jnp.minimum((r + 1) * hb, last_halo)
    vec = lambda: pl.BlockSpec((None, 1, CONV_CH), lambda r: (l, 0, 0))
    return pl.pallas_call(
        _conv_kernel,
        grid=(N_BLK,),
        in_specs=[pl.BlockSpec((BLK, CONV_CH), lambda r: (r, ca)),
                  pl.BlockSpec((BLK, CONV_CH), lambda r: (r, cg)),
                  pl.BlockSpec((CONV_HALO, CONV_CH), lambda r: (prev(r), ca)),
                  pl.BlockSpec((CONV_HALO, CONV_CH), lambda r: (prev(r), cg)),
                  pl.BlockSpec((CONV_HALO, CONV_CH), lambda r: (nxt(r), ca)),
                  pl.BlockSpec((CONV_HALO, CONV_CH), lambda r: (nxt(r), cg)),
                  pl.BlockSpec((None, CONV_K, CONV_CH), lambda r: (l, 0, 0)),
                  vec(), vec(), vec()],
        out_specs=pl.BlockSpec((BLK, CONV_CH), lambda r: (r, 0)),
        out_shape=jax.ShapeDtypeStruct((T, CONV_CH), BF16),
        scratch_shapes=[pltpu.VMEM((BLK + 2 * CONV_HALO, CONV_CH), F32),
                        pltpu.VMEM((SUB, BLK + 3 * SUB, CONV_CH), F32)],
        compiler_params=_cp("parallel"),
        name="conv",
    )(z, z, z, z, z, z, conv_w, conv_b.reshape(DEPTH, 1, CONV_CH),
      ln_g.reshape(DEPTH, 1, CONV_CH), ln_b.reshape(DEPTH, 1, CONV_CH))


def _norm_router_kernel(x_ref, g_ref, mod_ref, rw_ref, rb_ref, h_ref, idx_ref, gate_ref):
    y = _rms(x_ref[...], g_ref[...])
    h = y * (1.0 + mod_ref[4:5, :]) + mod_ref[3:4, :]
    h_ref[...] = h
    hb = h.astype(BF16)
    h_lo = (h - hb.astype(F32)).astype(BF16)
    w = rw_ref[...]
    wb = w.astype(BF16)
    w_lo = (w - wb.astype(F32)).astype(BF16)
    dot = lambda a, b: jnp.dot(a, b, preferred_element_type=F32)
    logits = dot(hb, wb) + (dot(hb, w_lo) + dot(h_lo, wb)) + rb_ref[...]
    lane = lax.broadcasted_iota(jnp.int32, logits.shape, 1)
    logits = jnp.where(lane < N_EXP, logits, -jnp.inf)
    col = lax.broadcasted_iota(jnp.int32, idx_ref.shape, 1)
    idx_out = jnp.zeros(idx_ref.shape, jnp.int32)
    val_out = jnp.zeros(idx_ref.shape, F32)
    cur = logits
    for k in range(TOP_K):
        mx = jnp.max(cur, axis=-1, keepdims=True)
        ix = jnp.min(jnp.where(cur == mx, lane, N_EXP), axis=-1, keepdims=True)
        idx_out = jnp.where(col == k, ix, idx_out)
        val_out = jnp.where(col == k, mx, val_out)
        cur = jnp.where(lane == ix, -jnp.inf, cur)
    gate_ref[...] = _softmax(val_out)
    idx_ref[...] = idx_out


def norm_router(x, g, mod, router_w, router_b, l):
    tm = 512
    sd = jax.ShapeDtypeStruct
    return pl.pallas_call(
        _norm_router_kernel,
        grid=(T // tm,),
        in_specs=[pl.BlockSpec((tm, D), lambda m: (m, 0)),
                  pl.BlockSpec((None, 1, D), lambda m: (l, 0, 0)),
                  pl.BlockSpec((None, None, 6, D), lambda m: (l, _grp_of_row(m * tm), 0, 0)),
                  pl.BlockSpec((None, D, LANE), lambda m: (l, 0, 0)),
                  pl.BlockSpec((None, 1, LANE), lambda m: (l, 0, 0))],
        out_specs=[pl.BlockSpec((tm, D), lambda m: (m, 0)),
                   pl.BlockSpec((tm, TOP_K), lambda m: (m, 0)),
                   pl.BlockSpec((tm, TOP_K), lambda m: (m, 0))],
        out_shape=[sd((T, D), F32), sd((T, TOP_K), jnp.int32), sd((T, TOP_K), F32)],
        compiler_params=_cp("parallel"),
        name="norm_router",
    )(x, g.reshape(DEPTH, 1, D), mod,
      jnp.pad(router_w, ((0, 0), (0, 0), (0, LANE - N_EXP))),
      jnp.pad(router_b.reshape(DEPTH, 1, N_EXP), ((0, 0), (0, 0), (0, LANE - N_EXP))))


def _moe_kernel(be_ref, cnt_ref, nu_ref, x_ref, wg_ref, wu_ref, bg_ref, bu_ref, wo_ref, bo_ref,
                o_ref, xb_ref, act_ref, wgb_ref, wub_ref, wob_ref):
    del be_ref
    m = pl.program_id(0)
    j = pl.program_id(1)
    dot = lambda a, b: jnp.dot(a, b, preferred_element_type=F32)

    @pl.when(m < nu_ref[0])
    def _():
        nsub = (cnt_ref[m] + (MOE_SUB - 1)) // MOE_SUB

        @pl.when(j == 0)
        def _():
            def cast(s, _):
                r0 = pl.multiple_of(s * MOE_SUB, MOE_SUB)
                xb_ref[pl.ds(r0, MOE_SUB), :] = x_ref[pl.ds(r0, MOE_SUB), :].astype(BF16)
                return 0

            lax.fori_loop(0, nsub, cast, 0)

        @pl.when(j < MOE_NF)
        def _():
            wgb_ref[...] = wg_ref[...].astype(BF16)
            wub_ref[...] = wu_ref[...].astype(BF16)

            def sub(s, _):
                r0 = pl.multiple_of(s * MOE_SUB, MOE_SUB)
                x = xb_ref[pl.ds(r0, MOE_SUB), :]
                g = jnp.minimum(dot(x, wgb_ref[...]) + bg_ref[...], LIMIT)
                up = jnp.clip(dot(x, wub_ref[...]) + bu_ref[...], -LIMIT, LIMIT)
                act = (up + 1.0) * (g * jax.nn.sigmoid(ALPHA * g))
                act_ref[jnp.minimum(j, MOE_NF - 1), pl.ds(r0, MOE_SUB), :] = act.astype(BF16)
                return 0

            lax.fori_loop(0, nsub, sub, 0)

        @pl.when(j >= MOE_NF)
        def _():
            wob_ref[...] = wo_ref[...].astype(BF16)

            def sub(s, _):
                r0 = pl.multiple_of(s * MOE_SUB, MOE_SUB)
                y = bo_ref[...] + dot(act_ref[0, pl.ds(r0, MOE_SUB), :], wob_ref[0:MOE_TF, :])
                for f in range(1, MOE_NF):
                    y = y + dot(act_ref[f, pl.ds(r0, MOE_SUB), :],
                                wob_ref[f * MOE_TF:(f + 1) * MOE_TF, :])
                o_ref[pl.ds(r0, MOE_SUB), :] = y
                return 0

            lax.fori_loop(0, nsub, sub, 0)

            def zero(s, _):
                r0 = pl.multiple_of(s * MOE_SUB, MOE_SUB)
                o_ref[pl.ds(r0, MOE_SUB), :] = jnp.zeros((MOE_SUB, MOE_TN), F32)
                return 0

            lax.fori_loop(nsub, MOE_BM // MOE_SUB, zero, 0)


def moe_experts(xs, block_e, block_cnt, n_used, w_in, b_in, w_out, b_out, l):
    def mm(m, nu):
        return jnp.minimum(m, nu[0] - 1)

    def f1(m, j, nu):
        return jnp.where(m < nu[0], jnp.minimum(j, MOE_NF - 1), MOE_NF - 1)

    def n1(m, j, nu):
        return jnp.where(m < nu[0], jnp.clip(j - MOE_NF, 0, MOE_NN - 1), MOE_NN - 1)

    grid_spec = pltpu.PrefetchScalarGridSpec(
        num_scalar_prefetch=3,
        grid=(MOE_BLOCKS, MOE_NF + MOE_NN),
        in_specs=[
            pl.BlockSpec((MOE_BM, D), lambda m, j, be, cnt, nu: (mm(m, nu), 0)),
            pl.BlockSpec((None, None, D, MOE_TF),
                         lambda m, j, be, cnt, nu: (l, be[m], 0, f1(m, j, nu))),
            pl.BlockSpec((None, None, D, MOE_TF),
                         lambda m, j, be, cnt, nu: (l, be[m], 0, MOE_NF + f1(m, j, nu))),
            pl.BlockSpec((None, None, 1, MOE_TF),
                         lambda m, j, be, cnt, nu: (l, be[m], 0, f1(m, j, nu))),
            pl.BlockSpec((None, None, 1, MOE_TF),
                         lambda m, j, be, cnt, nu: (l, be[m], 0, MOE_NF + f1(m, j, nu))),
            pl.BlockSpec((None, None, FF, MOE_TN),
                         lambda m, j, be, cnt, nu: (l, be[m], 0, n1(m, j, nu))),
            pl.BlockSpec((None, None, 1, MOE_TN),
                         lambda m, j, be, cnt, nu: (l, be[m], 0, n1(m, j, nu))),
        ],
        out_specs=pl.BlockSpec((MOE_BM, MOE_TN), lambda m, j, be, cnt, nu: (mm(m, nu), n1(m, j, nu))),
        scratch_shapes=[pltpu.VMEM((MOE_BM, D), BF16),
                        pltpu.VMEM((MOE_NF, MOE_BM, MOE_TF), BF16),
                        pltpu.VMEM((D, MOE_TF), BF16), pltpu.VMEM((D, MOE_TF), BF16),
                        pltpu.VMEM((FF, MOE_TN), BF16)],
    )
    return pl.pallas_call(
        _moe_kernel,
        grid_spec=grid_spec,
        out_shape=jax.ShapeDtypeStruct((MOE_ROWS, D), F32),
        compiler_params=_cp("arbitrary", "arbitrary", vmem=60 << 20),
        name="moe_experts",
    )(block_e, block_cnt, n_used, xs, w_in, w_in,
      b_in.reshape(DEPTH, N_EXP, 1, 2 * FF), b_in.reshape(DEPTH, N_EXP, 1, 2 * FF),
      w_out, b_out.reshape(DEPTH, N_EXP, 1, D))


def _combine_kernel(x_ref, y0_ref, y1_ref, y2_ref, y3_ref, gate_ref, mod_ref, o_ref):
    gate = gate_ref[...]
    acc = gate[:, 0:1] * y0_ref[...]
    for k, y_ref in enumerate((y1_ref, y2_ref, y3_ref), start=1):
        acc = acc + gate[:, k:k + 1] * y_ref[...]
    o_ref[...] = x_ref[...] + mod_ref[5:6, :] * acc


def moe_combine(x, yg, gate, mod, l):
    tm = 256
    nm = T // tm
    yspec = lambda k: pl.BlockSpec((tm, D), lambda m: (k * nm + m, 0))
    return pl.pallas_call(
        _combine_kernel,
        grid=(nm,),
        in_specs=[pl.BlockSpec((tm, D), lambda m: (m, 0)),
                  yspec(0), yspec(1), yspec(2), yspec(3),
                  pl.BlockSpec((tm, TOP_K), lambda m: (m, 0)),
                  pl.BlockSpec((None, None, 6, D), lambda m: (l, _grp_of_row(m * tm), 0, 0))],
        out_specs=pl.BlockSpec((tm, D), lambda m: (m, 0)),
        out_shape=jax.ShapeDtypeStruct((T, D), F32),
        compiler_params=_cp("parallel"),
        name="moe_combine",
    )(x, yg, yg, yg, yg, gate, mod)


def moe_routing(top_idx):
    n_assign = T * TOP_K
    flat_e = top_idx.reshape(-1)
    onehot = (flat_e[:, None] == jnp.arange(N_EXP, dtype=jnp.int32)[None, :]).astype(jnp.int32)
    csum = jnp.cumsum(onehot, axis=0)
    rank = jnp.take_along_axis(csum, flat_e[:, None], axis=1)[:, 0] - 1
    counts = csum[-1]
    nblk = (counts + MOE_BM - 1) // MOE_BM
    blk_end = jnp.cumsum(nblk)
    blk_start = blk_end - nblk
    n_used = blk_end[-1]
    dest = blk_start[flat_e] * MOE_BM + rank
    row_tok = jnp.zeros((MOE_ROWS,), jnp.int32).at[dest].set(
        jnp.arange(n_assign, dtype=jnp.int32) // TOP_K)
    mids = jnp.minimum(jnp.arange(MOE_BLOCKS, dtype=jnp.int32), n_used - 1)
    block_e = jnp.minimum(jnp.searchsorted(blk_end, mids, side='right'), N_EXP - 1).astype(jnp.int32)
    block_cnt = jnp.clip(counts[block_e] - (mids - blk_start[block_e]) * MOE_BM, 0, MOE_BM)
    block_cnt = jnp.where(jnp.arange(MOE_BLOCKS) < n_used, block_cnt, 0).astype(jnp.int32)
    return row_tok, dest, block_e, block_cnt, n_used.reshape(1).astype(jnp.int32)


def trunk_layer(x, l, mod, p, ssm_p, rope, cache_k, cache_v, h0):
    lambda_init = 0.8 - 0.6 * math.exp(-0.3 * l)
    zero_b = lambda n: jnp.zeros((DEPTH, 1, n), F32)

    h = norm_mod(x, p['norm_mix'], mod, l)
    z = matmul(h, p['w_in'], zero_b(IN_W), l, name="w_in")

    att = jnp.concatenate(
        [attn_ctx(z, p['diff_lambda'], p['diff_subln'], l, lambda_init),
         attn_lat(z, cache_k, cache_v, rope[0], rope[1], p['diff_lambda'], p['diff_subln'],
                  l, lambda_init)], axis=0)
    att_out = matmul(att, p['w_attn_out'], zero_b(D), l, out_dtype=BF16, name="attn_out")

    ab, bd, cd = ssm_p
    u_tm = jnp.transpose(z.reshape(N_BLK, BLK, IN_W)[:, :, 3 * QKV_W:3 * QKV_W + SSM_W], (1, 0, 2))
    y_tm, fin = ssm_mixer(u_tm, bd, cd, ab, p['ssm_d'].reshape(DEPTH, N_CC, 1, U_CH), h0, l)
    y = jnp.transpose(y_tm, (1, 0, 2)).reshape(T, SSM_W).astype(BF16)
    ssm_out = matmul_glu(y, p['w_ssm_glu'], l)

    cnv = conv_module(z, p['conv_w'], p['conv_b'], p['conv_ln_g'], p['conv_ln_b'], l)
    conv_out = matmul(cnv, p['w_conv_out'], p['b_conv_out'].reshape(DEPTH, 1, D), l,
                      out_dtype=BF16, name="conv_out")

    mixed = merge_gates(h, p['w_merge_gate'], p['b_merge_gate'].reshape(DEPTH, 1, 3 * D),
                        att_out, ssm_out, conv_out, l)
    x = matmul_resid(mixed, p['w_out'], x, mod, l, gate_row=2)

    h2p, top_idx, gate = norm_router(x, p['norm_ffn'], mod, p['router_w'], p['router_b'], l)
    row_tok, dest, block_e, block_cnt, n_used = moe_routing(top_idx)
    xs = h2p.at[row_tok].get(mode='promise_in_bounds')
    ys = moe_experts(xs, block_e, block_cnt, n_used,
                     p['moe_w_in'], p['moe_b_in'], p['moe_w_out'], p['moe_b_out'], l)
    dest_k = dest.reshape(T, TOP_K).T.reshape(-1)
    yg = ys.at[dest_k].get(mode='promise_in_bounds')
    x = moe_combine(x, yg, gate, mod, l)

    k_new = z[:T_CTX, QKV_W:2 * QKV_W].reshape(BATCH, SEQ, 2 * HEADS, HEAD_DIM)
    v_new = z[:T_CTX, 2 * QKV_W:3 * QKV_W].reshape(BATCH, SEQ, HEADS, 2 * HEAD_DIM)
    s_new = jnp.transpose(fin[:, :, :BATCH, :], (2, 0, 1, 3)).reshape(BATCH, 2, 2, SSM_G, SSM_N)
    return x, k_new, v_new, s_new


def kernel(x_prompt, x_sample, cache_k, cache_v, state_ssm, c, c_ctx, w_ada, b_ada, norm_mix, norm_ffn, w_in, diff_lambda, diff_subln, w_attn_out, ssm_a_re, ssm_a_im, ssm_log_dt, ssm_b, ssm_c, ssm_d, w_ssm_glu, conv_w, conv_b, conv_ln_g, conv_ln_b, w_conv_out, b_conv_out, w_merge_gate, b_merge_gate, w_out, router_w, router_b, moe_w_in, moe_b_in, moe_w_out, moe_b_out, norm_final):
    p = dict(norm_mix=norm_mix, norm_ffn=norm_ffn, w_in=w_in, diff_lambda=diff_lambda,
             diff_subln=diff_subln, w_attn_out=w_attn_out, ssm_d=ssm_d, w_ssm_glu=w_ssm_glu,
             conv_w=conv_w, conv_b=conv_b, conv_ln_g=conv_ln_g, conv_ln_b=conv_ln_b,
             w_conv_out=w_conv_out, b_conv_out=b_conv_out, w_merge_gate=w_merge_gate,
             b_merge_gate=b_merge_gate, w_out=w_out, router_w=router_w, router_b=router_b,
             moe_w_in=moe_w_in, moe_b_in=moe_b_in, moe_w_out=moe_w_out, moe_b_out=moe_b_out)

    cond8 = jnp.zeros((SUB, D), F32).at[0].set(c_ctx).at[1:1 + DEC_BATCH].set(c)
    mod = ada_all(cond8, w_ada, b_ada).reshape(DEPTH, SUB, 6, D)

    ab_re, ab_im, bb_re, bb_im = ssm_discretize(ssm_a_re, ssm_a_im, ssm_log_dt, ssm_b)
    ssm_p = ssm_weights(ab_re, ab_im, bb_re, bb_im, ssm_c)
    rope = rope_tables()

    ck = cache_k.reshape(DEC_BATCH, DEPTH, PAST, QKV_W)
    cv = cache_v.reshape(DEC_BATCH, DEPTH, PAST, QKV_W)
    st = state_ssm.reshape(DEC_BATCH, DEPTH, 2, 2, 1, SSM_G * SSM_N)
    h0_all = jnp.concatenate([jnp.zeros((N_SG - DEC_BATCH,) + st.shape[1:], F32), st], axis=0)

    x = jnp.concatenate([x_prompt.reshape(T_CTX, D), x_sample.reshape(T_LAT, D)], axis=0)
    new_k, new_v, new_s = [], [], []
    for l in range(DEPTH):
        x, k_l, v_l, s_l = trunk_layer(x, l, mod, p, ssm_p, rope, ck, cv, h0_all[:, l])
        new_k.append(k_l)
        new_v.append(v_l)
        new_s.append(s_l)
    y = final_norm(x, norm_final)
    return (y[:T_CTX].reshape(BATCH, SEQ, D), y[T_CTX:].reshape(DEC_BATCH, DEC_SEQ, D),
            jnp.stack(new_k, axis=1), jnp.stack(new_v, axis=1), jnp.stack(new_s, axis=1))
```

```python
import functools
import math

import jax
import jax.numpy as jnp
from jax import lax
from jax.experimental import pallas as pl
from jax.experimental.pallas import tpu as pltpu

F32 = jnp.float32
BF16 = jnp.bfloat16

D = 2048
BATCH = 32
SEQ = 256
DEPTH = 4
DEC_BATCH = 2
DEC_SEQ = 2048
PAST = 512
GRID_W = 64
HEADS = 8
HEAD_DIM = 64
ROPE_PAIRS = 16
ROPE_BASE = 10000.0
QKV_W = 1024
SSM_W = 1024
SSM_G = 64
SSM_P = 16
SSM_N = 64
CONV_CH = 1024
CONV_K = 31
IN_W = 6144
N_EXP = 32
TOP_K = 4
FF = 2048
LIMIT = 7.0
ALPHA = 1.702
EPS = 1e-6

T_CTX = BATCH * SEQ
T_LAT = DEC_BATCH * DEC_SEQ
T = T_CTX + T_LAT
BLK = 256
N_BLK = T // BLK
CTX_BLKS = T_CTX // BLK
LAT_BLKS = DEC_SEQ // BLK
N_GRP = 1 + DEC_BATCH
SUB = 8
LANE = 128
STATE_CH = 1024
U_CH = 256
N_CC = SSM_W // U_CH
N_SG = N_BLK // SUB

MOE_BM = 1024
MOE_SUB = 256
MOE_TF = 256
MOE_NF = FF // MOE_TF
MOE_TN = 256
MOE_NN = D // MOE_TN
MOE_RUN = 2
MOE_RUNS = (T * TOP_K) // (MOE_RUN * MOE_BM) + N_EXP
MOE_BLOCKS = (T * TOP_K) // MOE_BM + N_EXP
MOE_ROWS = MOE_BLOCKS * MOE_BM

VMEM_LIMIT = 56 << 20


def _cp(*sem, vmem=VMEM_LIMIT):
    return pltpu.CompilerParams(dimension_semantics=sem, vmem_limit_bytes=vmem)


def _grp_of_row(row):
    return jnp.where(row < T_CTX, 0, 1 + (row - T_CTX) // DEC_SEQ)


def _ada_kernel(c_ref, w_ref, b_ref, o_ref):
    c = c_ref[...]
    a = (c * jax.nn.sigmoid(c)).astype(BF16)
    o_ref[...] = jnp.dot(a, w_ref[...].astype(BF16), preferred_element_type=F32) + b_ref[...]


def ada_all(cond8, w_ada, b_ada):
    tn = 1024
    n = 6 * D
    return pl.pallas_call(
        _ada_kernel,
        grid=(DEPTH, n // tn),
        in_specs=[pl.BlockSpec((SUB, D), lambda l, j: (0, 0)),
                  pl.BlockSpec((None, D, tn), lambda l, j: (l, 0, j)),
                  pl.BlockSpec((None, 1, tn), lambda l, j: (l, 0, j))],
        out_specs=pl.BlockSpec((None, SUB, tn), lambda l, j: (l, 0, j)),
        out_shape=jax.ShapeDtypeStruct((DEPTH, SUB, n), F32),
        compiler_params=_cp("parallel", "parallel"),
        name="ada",
    )(cond8, w_ada, b_ada.reshape(DEPTH, 1, n))


def _rms(x, g):
    return x * lax.rsqrt(jnp.mean(x * x, axis=-1, keepdims=True) + EPS) * g


def _norm_mod_kernel(x_ref, g_ref, mod_ref, o_ref):
    y = _rms(x_ref[...], g_ref[...])
    sh = mod_ref[0:1, :]
    sc = mod_ref[1:2, :]
    o_ref[...] = (y * (1.0 + sc) + sh).astype(o_ref.dtype)


def norm_mod(x, g, mod, l):
    tm = 512
    return pl.pallas_call(
        _norm_mod_kernel,
        grid=(T // tm,),
        in_specs=[pl.BlockSpec((tm, D), lambda m: (m, 0)),
                  pl.BlockSpec((None, 1, D), lambda m: (l, 0, 0)),
                  pl.BlockSpec((None, None, 6, D), lambda m: (l, _grp_of_row(m * tm), 0, 0))],
        out_specs=pl.BlockSpec((tm, D), lambda m: (m, 0)),
        out_shape=jax.ShapeDtypeStruct((T, D), BF16),
        compiler_params=_cp("parallel"),
        name="norm_mod",
    )(x, g.reshape(DEPTH, 1, D), mod)


def _final_norm_kernel(x_ref, g_ref, o_ref):
    o_ref[...] = _rms(x_ref[...], g_ref[...])


def final_norm(x, g):
    tm = 512
    return pl.pallas_call(
        _final_norm_kernel,
        grid=(T // tm,),
        in_specs=[pl.BlockSpec((tm, D), lambda m: (m, 0)),
                  pl.BlockSpec((1, D), lambda m: (0, 0))],
        out_specs=pl.BlockSpec((tm, D), lambda m: (m, 0)),
        out_shape=jax.ShapeDtypeStruct((T, D), F32),
        compiler_params=_cp("parallel"),
        name="final_norm",
    )(x, g.reshape(1, D))


def _mm_kernel(a_ref, w_ref, b_ref, o_ref, wb_ref):
    @pl.when(pl.program_id(1) == 0)
    def _():
        wb_ref[...] = w_ref[...].astype(BF16)

    acc = jnp.dot(a_ref[...], wb_ref[...], preferred_element_type=F32)
    o_ref[...] = (acc + b_ref[...]).astype(o_ref.dtype)


def matmul(a, w, bias, l, *, tm=1024, tn=1024, out_dtype=F32, name="mm"):
    k, n = w.shape[1], w.shape[2]
    return pl.pallas_call(
        _mm_kernel,
        grid=(n // tn, T // tm),
        in_specs=[pl.BlockSpec((tm, k), lambda j, m: (m, 0)),
                  pl.BlockSpec((None, k, tn), lambda j, m: (l, 0, j)),
                  pl.BlockSpec((None, 1, tn), lambda j, m: (l, 0, j))],
        out_specs=pl.BlockSpec((tm, tn), lambda j, m: (m, j)),
        out_shape=jax.ShapeDtypeStruct((T, n), out_dtype),
        scratch_shapes=[pltpu.VMEM((k, tn), BF16)],
        compiler_params=_cp("parallel", "arbitrary"),
        name=name,
    )(a, w, bias)


def _mm_glu_kernel(a_ref, wa_ref, wg_ref, o_ref, wab_ref, wgb_ref):
    @pl.when(pl.program_id(1) == 0)
    def _():
        wab_ref[...] = wa_ref[...].astype(BF16)
        wgb_ref[...] = wg_ref[...].astype(BF16)

    a = a_ref[...]
    va = jnp.dot(a, wab_ref[...], preferred_element_type=F32)
    vg = jnp.dot(a, wgb_ref[...], preferred_element_type=F32)
    o_ref[...] = (va * jax.nn.sigmoid(vg)).astype(o_ref.dtype)


def matmul_glu(a, w, l, *, tm=1024, tn=512):
    k, n = w.shape[1], w.shape[2] // 2
    nj = n // tn
    return pl.pallas_call(
        _mm_glu_kernel,
        grid=(nj, T // tm),
        in_specs=[pl.BlockSpec((tm, k), lambda j, m: (m, 0)),
                  pl.BlockSpec((None, k, tn), lambda j, m: (l, 0, j)),
                  pl.BlockSpec((None, k, tn), lambda j, m: (l, 0, nj + j))],
        out_specs=pl.BlockSpec((tm, tn), lambda j, m: (m, j)),
        out_shape=jax.ShapeDtypeStruct((T, n), BF16),
        scratch_shapes=[pltpu.VMEM((k, tn), BF16), pltpu.VMEM((k, tn), BF16)],
        compiler_params=_cp("parallel", "arbitrary"),
        name="ssm_glu",
    )(a, w, w)


def _mm_resid_kernel(a_ref, w_ref, x_ref, mod_ref, o_ref, wb_ref, *, gate_row):
    @pl.when(pl.program_id(1) == 0)
    def _():
        wb_ref[...] = w_ref[...].astype(BF16)

    acc = jnp.dot(a_ref[...], wb_ref[...], preferred_element_type=F32)
    o_ref[...] = x_ref[...] + mod_ref[gate_row:gate_row + 1, :] * acc


def matmul_resid(a, w, x, mod, l, *, gate_row, tm=1024, tn=1024):
    k, n = w.shape[1], w.shape[2]
    return pl.pallas_call(
        functools.partial(_mm_resid_kernel, gate_row=gate_row),
        grid=(n // tn, T // tm),
        in_specs=[pl.BlockSpec((tm, k), lambda j, m: (m, 0)),
                  pl.BlockSpec((None, k, tn), lambda j, m: (l, 0, j)),
                  pl.BlockSpec((tm, tn), lambda j, m: (m, j)),
                  pl.BlockSpec((None, None, 6, tn), lambda j, m: (l, _grp_of_row(m * tm), 0, j))],
        out_specs=pl.BlockSpec((tm, tn), lambda j, m: (m, j)),
        out_shape=jax.ShapeDtypeStruct((T, n), F32),
        scratch_shapes=[pltpu.VMEM((k, tn), BF16)],
        compiler_params=_cp("parallel", "arbitrary"),
        name="out_resid",
    )(a, w, x, mod)


def _merge_kernel(h_ref, w0_ref, w1_ref, w2_ref, b0_ref, b1_ref, b2_ref,
                  att_ref, ssm_ref, cnv_ref, o_ref, wb0_ref, wb1_ref, wb2_ref):
    @pl.when(pl.program_id(1) == 0)
    def _():
        wb0_ref[...] = w0_ref[...].astype(BF16)
        wb1_ref[...] = w1_ref[...].astype(BF16)
        wb2_ref[...] = w2_ref[...].astype(BF16)

    h = h_ref[...]

    def gate(wb_ref, b_ref):
        return jax.nn.sigmoid(jnp.dot(h, wb_ref[...], preferred_element_type=F32) + b_ref[...])

    mixed = (gate(wb0_ref, b0_ref) * att_ref[...] + gate(wb1_ref, b1_ref) * ssm_ref[...]
             + gate(wb2_ref, b2_ref) * cnv_ref[...])
    o_ref[...] = mixed.astype(o_ref.dtype)


def merge_gates(h, w, b, att, ssm, cnv, l, *, tm=512, tn=512):
    nj = D // tn
    wspec = lambda br: pl.BlockSpec((None, D, tn), lambda j, m: (l, 0, br * nj + j))
    bspec = lambda br: pl.BlockSpec((None, 1, tn), lambda j, m: (l, 0, br * nj + j))
    tile = pl.BlockSpec((tm, tn), lambda j, m: (m, j))
    return pl.pallas_call(
        _merge_kernel,
        grid=(nj, T // tm),
        in_specs=[pl.BlockSpec((tm, D), lambda j, m: (m, 0)),
                  wspec(0), wspec(1), wspec(2), bspec(0), bspec(1), bspec(2),
                  tile, tile, tile],
        out_specs=tile,
        out_shape=jax.ShapeDtypeStruct((T, D), BF16),
        scratch_shapes=[pltpu.VMEM((D, tn), BF16)] * 3,
        compiler_params=_cp("parallel", "arbitrary"),
        name="merge_gates",
    )(h, w, w, w, b, b, b, att, ssm, cnv)


def _lambda(lam_ref, lambda_init):
    lp = lam_ref[...]
    s01 = jnp.sum(lp[0:1, :] * lp[1:2, :], axis=-1, keepdims=True)
    s23 = jnp.sum(lp[2:3, :] * lp[3:4, :], axis=-1, keepdims=True)
    return jnp.exp(s01) - jnp.exp(s23) + lambda_init


def _softmax(s):
    e = jnp.exp(s - jnp.max(s, axis=-1, keepdims=True))
    return e * (1.0 / jnp.sum(e, axis=-1, keepdims=True))


def _diff_attn_head(q, kb, vb, lam, g, lambda_init):
    lane = lax.broadcasted_iota(jnp.int32, q.shape, 1)
    q_a = jnp.where(lane < HEAD_DIM, q, 0.0).astype(BF16)
    q_b = jnp.where(lane >= HEAD_DIM, q, 0.0).astype(BF16)
    nt = (((1,), (1,)), ((), ()))
    p_a = _softmax(lax.dot_general(q_a, kb, nt, preferred_element_type=F32))
    p_b = _softmax(lax.dot_general(q_b, kb, nt, preferred_element_type=F32))
    a = (p_a - lam * p_b).astype(BF16)
    o = jnp.dot(a, vb, preferred_element_type=F32)
    return _rms(o, g) * (1.0 - lambda_init)


def _attn_ctx_kernel(q_ref, k_ref, v_ref, lam_ref, g_ref, o_ref, *, lambda_init):
    lam = _lambda(lam_ref, lambda_init)
    g = g_ref[...]
    for h in range(HEADS):
        cols = slice(h * LANE, (h + 1) * LANE)
        q = q_ref[:, cols] * (HEAD_DIM ** -0.5)
        o = _diff_attn_head(q, k_ref[:, cols].astype(BF16), v_ref[:, cols].astype(BF16),
                            lam, g, lambda_init)
        o_ref[:, cols] = o.astype(o_ref.dtype)


def attn_ctx(z, lam_p, subln, l, lambda_init):
    return pl.pallas_call(
        functools.partial(_attn_ctx_kernel, lambda_init=lambda_init),
        grid=(BATCH,),
        in_specs=[pl.BlockSpec((SEQ, QKV_W), lambda b: (b, 0)),
                  pl.BlockSpec((SEQ, QKV_W), lambda b: (b, 1)),
                  pl.BlockSpec((SEQ, QKV_W), lambda b: (b, 2)),
                  pl.BlockSpec((None, 4, HEAD_DIM), lambda b: (l, 0, 0)),
                  pl.BlockSpec((None, 1, LANE), lambda b: (l, 0, 0))],
        out_specs=pl.BlockSpec((SEQ, QKV_W), lambda b: (b, 0)),
        out_shape=jax.ShapeDtypeStruct((T_CTX, QKV_W), BF16),
        compiler_params=_cp("parallel"),
        name="attn_ctx",
    )(z, z, z, lam_p, subln.reshape(DEPTH, 1, LANE))


def _rope(x, cos, sin_signed):
    lane = lax.broadcasted_iota(jnp.int32, x.shape, 1)
    first = (lane % (2 * ROPE_PAIRS)) < ROPE_PAIRS
    partner = jnp.where(first, pltpu.roll(x, LANE - ROPE_PAIRS, 1), pltpu.roll(x, ROPE_PAIRS, 1))
    return x * cos + partner * sin_signed


def _attn_lat_kernel(q_ref, k_ref, v_ref, ck_ref, cv_ref, cq_ref, sq_ref, ckk_ref, skk_ref,
                     lam_ref, g_ref, o_ref, kb_ref, vb_ref, *, lambda_init):
    @pl.when(pl.program_id(2) == 0)
    def _():
        kb_ref[0:DEC_SEQ, :] = _rope(k_ref[...], ckk_ref[...], skk_ref[...]).astype(BF16)
        kb_ref[DEC_SEQ:DEC_SEQ + PAST, :] = ck_ref[...].astype(BF16)
        vb_ref[0:DEC_SEQ, :] = v_ref[...].astype(BF16)
        vb_ref[DEC_SEQ:DEC_SEQ + PAST, :] = cv_ref[...].astype(BF16)

    q = _rope(q_ref[...], cq_ref[...], sq_ref[...]) * (HEAD_DIM ** -0.5)
    o = _diff_attn_head(q, kb_ref[...], vb_ref[...], _lambda(lam_ref, lambda_init), g_ref[...],
                        lambda_init)
    o_ref[...] = o.astype(o_ref.dtype)


def attn_lat(z, cache_k, cache_v, cos_t, sin_t, lam_p, subln, l, lambda_init):
    tq = BLK
    nq = DEC_SEQ // tq
    qrow = lambda b, h, i: CTX_BLKS + b * nq + i
    krow = lambda b: T_CTX // DEC_SEQ + b
    return pl.pallas_call(
        functools.partial(_attn_lat_kernel, lambda_init=lambda_init),
        grid=(DEC_BATCH, HEADS, nq),
        in_specs=[pl.BlockSpec((tq, LANE), lambda b, h, i: (qrow(b, h, i), h)),
                  pl.BlockSpec((DEC_SEQ, LANE), lambda b, h, i: (krow(b), HEADS + h)),
                  pl.BlockSpec((DEC_SEQ, LANE), lambda b, h, i: (krow(b), 2 * HEADS + h)),
                  pl.BlockSpec((None, None, PAST, LANE), lambda b, h, i: (b, l, 0, h)),
                  pl.BlockSpec((None, None, PAST, LANE), lambda b, h, i: (b, l, 0, h)),
                  pl.BlockSpec((tq, LANE), lambda b, h, i: (i, 0)),
                  pl.BlockSpec((tq, LANE), lambda b, h, i: (i, 0)),
                  pl.BlockSpec((DEC_SEQ, LANE), lambda b, h, i: (0, 0)),
                  pl.BlockSpec((DEC_SEQ, LANE), lambda b, h, i: (0, 0)),
                  pl.BlockSpec((None, 4, HEAD_DIM), lambda b, h, i: (l, 0, 0)),
                  pl.BlockSpec((None, 1, LANE), lambda b, h, i: (l, 0, 0))],
        out_specs=pl.BlockSpec((tq, LANE), lambda b, h, i: (b * nq + i, h)),
        out_shape=jax.ShapeDtypeStruct((T_LAT, QKV_W), BF16),
        scratch_shapes=[pltpu.VMEM((DEC_SEQ + PAST, LANE), BF16)] * 2,
        compiler_params=_cp("parallel", "parallel", "arbitrary"),
        name="attn_lat",
    )(z, z, z, cache_k, cache_v, cos_t, sin_t, cos_t, sin_t, lam_p,
      subln.reshape(DEPTH, 1, LANE))


def rope_tables():
    row = (jnp.arange(DEC_SEQ) // GRID_W).astype(F32)
    col = (jnp.arange(DEC_SEQ) % GRID_W).astype(F32)
    inv_freq = ROPE_BASE ** (-jnp.arange(ROPE_PAIRS, dtype=F32) / ROPE_PAIRS)
    ang = jnp.stack([row[:, None] * inv_freq, col[:, None] * inv_freq], axis=1)
    cos, sin = jnp.cos(ang), jnp.sin(ang)
    cos64 = jnp.concatenate([cos, cos], axis=-1).reshape(DEC_SEQ, HEAD_DIM)
    sin64 = jnp.concatenate([-sin, sin], axis=-1).reshape(DEC_SEQ, HEAD_DIM)
    return jnp.tile(cos64, (1, 2)), jnp.tile(sin64, (1, 2))


def _zoh_kernel(are_ref, aim_ref, ldt_ref, bre_ref, bim_ref,
                abr_ref, abi_ref, bbr_ref, bbi_ref):
    a_re = are_ref[...]
    a_im = aim_ref[...]
    dt = jnp.exp(ldt_ref[...])
    mag = jnp.exp(dt * a_re)
    ang = dt * a_im
    ab_re = mag * jnp.cos(ang)
    ab_im = mag * jnp.sin(ang)
    den = a_re * a_re + a_im * a_im
    xm = ab_re - 1.0
    f_re = (xm * a_re + ab_im * a_im) / den
    f_im = (ab_im * a_re - xm * a_im) / den
    abr_ref[...] = ab_re
    abi_ref[...] = ab_im
    rows = a_re.shape[0]
    fr = jnp.broadcast_to(f_re[:, None, :], (rows, SSM_P, SSM_N))
    fi = jnp.broadcast_to(f_im[:, None, :], (rows, SSM_P, SSM_N))
    b_re = bre_ref[...]
    b_im = bim_ref[...]
    bbr_ref[...] = fr * b_re - fi * b_im
    bbi_ref[...] = fr * b_im + fi * b_re


def ssm_discretize(a_re, a_im, log_dt, b):
    rows = DEPTH * 2 * SSM_G
    bt = jnp.swapaxes(b, -1, -2)
    b_re = bt[:, :, 0].reshape(rows, SSM_P, SSM_N)
    b_im = bt[:, :, 1].reshape(rows, SSM_P, SSM_N)
    sd = jax.ShapeDtypeStruct
    ab_re, ab_im, bb_re, bb_im = pl.pallas_call(
        _zoh_kernel,
        out_shape=[sd((rows, SSM_N), F32), sd((rows, SSM_N), F32),
                   sd((rows, SSM_P, SSM_N), F32), sd((rows, SSM_P, SSM_N), F32)],
        compiler_params=pltpu.CompilerParams(vmem_limit_bytes=VMEM_LIMIT),
        name="ssm_zoh",
    )(a_re.reshape(rows, SSM_N), a_im.reshape(rows, SSM_N), log_dt.reshape(rows, 1), b_re, b_im)
    return ab_re, ab_im, bb_re, bb_im


def _block_diag(m):
    eye = jnp.eye(16, dtype=m.dtype)
    out = m[..., :, :, None, :] * eye[:, None, :, None]
    return out.reshape(*m.shape[:-3], 16 * m.shape[-2], 16 * m.shape[-1])


def ssm_weights(ab_re, ab_im, bb_re, bb_im, c):
    ab = jnp.stack([ab_re, ab_im], axis=0).reshape(2, DEPTH, 2, N_CC, 1, STATE_CH)
    ab = jnp.transpose(ab, (1, 2, 0, 3, 4, 5))
    bb = jnp.stack([bb_re, bb_im], axis=0).reshape(2, DEPTH, 2, N_CC, 16, SSM_P, SSM_N)
    bd = _block_diag(jnp.transpose(bb, (1, 2, 0, 3, 4, 5, 6))).astype(BF16)
    ct = jnp.swapaxes(c, -1, -2).reshape(DEPTH, 2, 2, N_CC, 16, SSM_N, SSM_P)
    cd = _block_diag(ct).astype(BF16)
    return ab, bd, cd


def _cmul_add(ar, ai, xr, xi, br, bi):
    return ar * xr - ai * xi + br, ar * xi + ai * xr + bi


def _ssm_kernel(u_ref, bd_ref, cd_ref, ab_ref, d_ref, h0_ref, y_ref, fin_ref,
                xr_ref, xi_ref, tr_ref, ti_ref, acc_ref, cin_ref):
    nj = BLK
    is_lat = pl.program_id(0) >= CTX_BLKS // SUB
    u = u_ref[...].reshape(nj * SUB, U_CH)
    ub = u.astype(BF16)
    acc_ref[...] = u * d_ref[...]
    for d in range(2):
        xr_ref[...] = jnp.dot(ub, bd_ref[d, 0], preferred_element_type=F32)
        xi_ref[...] = jnp.dot(ub, bd_ref[d, 1], preferred_element_type=F32)
        a1r = ab_ref[d, 0]
        a1i = ab_ref[d, 1]
        ar = jnp.broadcast_to(a1r, (SUB, STATE_CH))
        ai = jnp.broadcast_to(a1i, (SUB, STATE_CH))

        def step(i, carry, d=d, ar=ar, ai=ai):
            j = i if d == 0 else nj - 1 - i
            r0 = pl.multiple_of(j * SUB, SUB)
            nr, ni = _cmul_add(ar, ai, carry[0], carry[1],
                               xr_ref[pl.ds(r0, SUB), :], xi_ref[pl.ds(r0, SUB), :])
            xr_ref[pl.ds(r0, SUB), :] = nr
            xi_ref[pl.ds(r0, SUB), :] = ni
            return nr, ni

        zero = jnp.zeros((SUB, STATE_CH), F32)
        lax.fori_loop(0, nj, step, (zero, zero), unroll=4)

        last = (nj - 1) * SUB if d == 0 else 0
        fin_ref[d, 0] = xr_ref[last:last + SUB, :]
        fin_ref[d, 1] = xi_ref[last:last + SUB, :]

        @pl.when(is_lat)
        def _(d=d, a1r=a1r, a1i=a1i, last=last):
            def pstep(j, p):
                tr_ref[pl.ds(j, 1), :] = p[0]
                ti_ref[pl.ds(j, 1), :] = p[1]
                return _cmul_add(a1r, a1i, p[0], p[1], 0.0, 0.0)

            lax.fori_loop(0, nj, pstep, (a1r, a1i))
            anr = tr_ref[nj - 1:nj, :]
            ani = ti_ref[nj - 1:nj, :]
            cr = [None] * SUB
            ci = [None] * SUB
            order = list(range(SUB)) if d == 0 else list(range(SUB - 1, -1, -1))
            cr[order[0]] = h0_ref[d, 0]
            ci[order[0]] = h0_ref[d, 1]
            for prev, cur in zip(order[:-1], order[1:]):
                fr = xr_ref[last + prev:last + prev + 1, :]
                fi = xi_ref[last + prev:last + prev + 1, :]
                cr[cur], ci[cur] = _cmul_add(anr, ani, cr[prev], ci[prev], fr, fi)
            for s in range(SUB):
                cin_ref[0, s:s + 1, :] = cr[s]
                cin_ref[1, s:s + 1, :] = ci[s]
            cin_r = cin_ref[0]
            cin_i = cin_ref[1]

            def fstep(i, _):
                t = i if d == 0 else nj - 1 - i
                r0 = pl.multiple_of(i * SUB, SUB)
                pr = tr_ref[pl.ds(t, 1), :]
                pi = ti_ref[pl.ds(t, 1), :]
                nr, ni = _cmul_add(pr, pi, cin_r, cin_i,
                                   xr_ref[pl.ds(r0, SUB), :], xi_ref[pl.ds(r0, SUB), :])
                xr_ref[pl.ds(r0, SUB), :] = nr
                xi_ref[pl.ds(r0, SUB), :] = ni
                return 0

            lax.fori_loop(0, nj, fstep, 0, unroll=2)

        acc_ref[...] += (jnp.dot(xr_ref[...].astype(BF16), cd_ref[d, 0], preferred_element_type=F32)
                         - jnp.dot(xi_ref[...].astype(BF16), cd_ref[d, 1], preferred_element_type=F32))

    y_ref[...] = jax.nn.gelu(acc_ref[...]).reshape(nj, SUB, U_CH)


def ssm_mixer(u_tm, bd, cd, ab, dvec, h0, l):
    nrow = BLK * SUB
    sd = jax.ShapeDtypeStruct
    return pl.pallas_call(
        _ssm_kernel,
        grid=(N_SG, N_CC),
        in_specs=[pl.BlockSpec((BLK, SUB, U_CH), lambda s, c: (0, s, c)),
                  pl.BlockSpec((None, 2, 2, None, U_CH, STATE_CH), lambda s, c: (l, 0, 0, c, 0, 0)),
                  pl.BlockSpec((None, 2, 2, None, STATE_CH, U_CH), lambda s, c: (l, 0, 0, c, 0, 0)),
                  pl.BlockSpec((None, 2, 2, None, 1, STATE_CH), lambda s, c: (l, 0, 0, c, 0, 0)),
                  pl.BlockSpec((None, None, 1, U_CH), lambda s, c: (l, c, 0, 0)),
                  pl.BlockSpec((None, 2, 2, 1, STATE_CH), lambda s, c: (s, 0, 0, 0, c))],
        out_specs=[pl.BlockSpec((BLK, SUB, U_CH), lambda s, c: (0, s, c)),
                   pl.BlockSpec((2, 2, SUB, STATE_CH), lambda s, c: (0, 0, s, c))],
        out_shape=[sd((BLK, N_BLK, SSM_W), F32), sd((2, 2, N_BLK, SSM_G * SSM_N), F32)],
        scratch_shapes=[pltpu.VMEM((nrow, STATE_CH), F32), pltpu.VMEM((nrow, STATE_CH), F32),
                        pltpu.VMEM((BLK, STATE_CH), F32), pltpu.VMEM((BLK, STATE_CH), F32),
                        pltpu.VMEM((nrow, U_CH), F32), pltpu.VMEM((2, SUB, STATE_CH), F32)],
        compiler_params=_cp("parallel", "parallel"),
        name="ssm",
    )(u_tm, bd, cd, ab, dvec, h0)


CONV_HALO = 16
CONV_RC = 32


def _conv_kernel(xa_ref, xg_ref, pa_ref, pg_ref, na_ref, ng_ref, w_ref, cb_ref, lg_ref, lb_ref,
                 o_ref, xp_ref, xs_ref):
    r = pl.program_id(0)
    is_lat = r >= CTX_BLKS
    pos = jnp.bitwise_and(r, LAT_BLKS - 1)
    pv = jnp.where(is_lat & (pos != 0), 1.0, 0.0)
    nv = jnp.where(is_lat & (pos != LAT_BLKS - 1), 1.0, 0.0)
    glu = lambda a_ref, g_ref: a_ref[...] * jax.nn.sigmoid(g_ref[...])
    xp_ref[0:CONV_HALO, :] = glu(pa_ref, pg_ref) * pv
    xp_ref[CONV_HALO:CONV_HALO + BLK, :] = glu(xa_ref, xg_ref)
    xp_ref[CONV_HALO + BLK:, :] = glu(na_ref, ng_ref) * nv
    span = BLK + 3 * SUB
    for s in range(SUB):
        xs_ref[s] = xp_ref[s:s + span, :]

    def chunk(c, _):
        r0 = pl.multiple_of(c * CONV_RC, CONV_RC)
        acc = jnp.zeros((CONV_RC, CONV_CH), F32)
        for k in range(CONV_K):
            off = k + CONV_HALO - CONV_K // 2
            acc = acc + xs_ref[off % SUB, pl.ds(r0 + (off // SUB) * SUB, CONV_RC), :] * w_ref[k:k + 1, :]
        acc = acc + cb_ref[...]
        mu = jnp.mean(acc, axis=-1, keepdims=True)
        xc = acc - mu
        var = jnp.mean(xc * xc, axis=-1, keepdims=True)
        yn = xc * lax.rsqrt(var + EPS) * lg_ref[...] + lb_ref[...]
        o_ref[pl.ds(r0, CONV_RC), :] = (yn * jax.nn.sigmoid(yn)).astype(o_ref.dtype)
        return 0

    lax.fori_loop(0, BLK // CONV_RC, chunk, 0)


def conv_module(z, conv_w, conv_b, ln_g, ln_b, l):
    hb = BLK // CONV_HALO
    last_halo = T // CONV_HALO - 1
    ca, cg = 4, 5
    prev = lambda r: jnp.maximum(r * hb - 1, 0)
    nxt = lambda r: jnp.minimum((r + 1) * hb, last_halo)
    vec = lambda: pl.BlockSpec((None, 1, CONV_CH), lambda r: (l, 0, 0))
    return pl.pallas_call(
        _conv_kernel,
        grid=(N_BLK,),
        in_specs=[pl.BlockSpec((BLK, CONV_CH), lambda r: (r, ca)),
                  pl.BlockSpec((BLK, CONV_CH), lambda r: (r, cg)),
                  pl.BlockSpec((CONV_HALO, CONV_CH), lambda r: (prev(r), ca)),
                  pl.BlockSpec((CONV_HALO, CONV_CH), lambda r: (prev(r), cg)),
                  pl.BlockSpec((CONV_HALO, CONV_CH), lambda r: (nxt(r), ca)),
                  pl.BlockSpec((CONV_HALO, CONV_CH), lambda r: (nxt(r), cg)),
                  pl.BlockSpec((None, CONV_K, CONV_CH), lambda r: (l, 0, 0)),
                  vec(), vec(), vec()],
        out_specs=pl.BlockSpec((BLK, CONV_CH), lambda r: (r, 0)),
        out_shape=jax.ShapeDtypeStruct((T, CONV_CH), BF16),
        scratch_shapes=[pltpu.VMEM((BLK + 2 * CONV_HALO, CONV_CH), F32),
                        pltpu.VMEM((SUB, BLK + 3 * SUB, CONV_CH), F32)],
        compiler_params=_cp("parallel"),
        name="conv",
    )(z, z, z, z, z, z, conv_w, conv_b.reshape(DEPTH, 1, CONV_CH),
      ln_g.reshape(DEPTH, 1, CONV_CH), ln_b.reshape(DEPTH, 1, CONV_CH))


def _norm_router_kernel(x_ref, g_ref, mod_ref, rw_ref, rb_ref, h_ref, idx_ref, gate_ref):
    y = _rms(x_ref[...], g_ref[...])
    h = y * (1.0 + mod_ref[4:5, :]) + mod_ref[3:4, :]
    h_ref[...] = h
    hb = h.astype(BF16)
    h_lo = (h - hb.astype(F32)).astype(BF16)
    w = rw_ref[...]
    wb = w.astype(BF16)
    w_lo = (w - wb.astype(F32)).astype(BF16)
    dot = lambda a, b: jnp.dot(a, b, preferred_element_type=F32)
    logits = dot(hb, wb) + (dot(hb, w_lo) + dot(h_lo, wb)) + rb_ref[...]
    lane = lax.broadcasted_iota(jnp.int32, logits.shape, 1)
    logits = jnp.where(lane < N_EXP, logits, -jnp.inf)
    col = lax.broadcasted_iota(jnp.int32, idx_ref.shape, 1)
    idx_out = jnp.zeros(idx_ref.shape, jnp.int32)
    val_out = jnp.zeros(idx_ref.shape, F32)
    cur = logits
    for k in range(TOP_K):
        mx = jnp.max(cur, axis=-1, keepdims=True)
        ix = jnp.min(jnp.where(cur == mx, lane, N_EXP), axis=-1, keepdims=True)
        idx_out = jnp.where(col == k, ix, idx_out)
        val_out = jnp.where(col == k, mx, val_out)
        cur = jnp.where(lane == ix, -jnp.inf, cur)
    gate_ref[...] = _softmax(val_out)
    idx_ref[...] = idx_out


def norm_router(x, g, mod, router_w, router_b, l):
    tm = 512
    sd = jax.ShapeDtypeStruct
    return pl.pallas_call(
        _norm_router_kernel,
        grid=(T // tm,),
        in_specs=[pl.BlockSpec((tm, D), lambda m: (m, 0)),
                  pl.BlockSpec((None, 1, D), lambda m: (l, 0, 0)),
                  pl.BlockSpec((None, None, 6, D), lambda m: (l, _grp_of_row(m * tm), 0, 0)),
                  pl.BlockSpec((None, D, LANE), lambda m: (l, 0, 0)),
                  pl.BlockSpec((None, 1, LANE), lambda m: (l, 0, 0))],
        out_specs=[pl.BlockSpec((tm, D), lambda m: (m, 0)),
                   pl.BlockSpec((tm, TOP_K), lambda m: (m, 0)),
                   pl.BlockSpec((tm, TOP_K), lambda m: (m, 0))],
        out_shape=[sd((T, D), F32), sd((T, TOP_K), jnp.int32), sd((T, TOP_K), F32)],
        compiler_params=_cp("parallel"),
        name="norm_router",
    )(x, g.reshape(DEPTH, 1, D), mod,
      jnp.pad(router_w, ((0, 0), (0, 0), (0, LANE - N_EXP))),
      jnp.pad(router_b.reshape(DEPTH, 1, N_EXP), ((0, 0), (0, 0), (0, LANE - N_EXP))))


def _moe_kernel(re_ref, b0_ref, nb_ref, cnt_ref, nr_ref, x_ref, wg_ref, wu_ref, bg_ref, bu_ref,
                wo_ref, bo_ref, o_ref, xb_ref, act_ref, wgb_ref, wub_ref, wob_ref):
    del re_ref
    r = pl.program_id(0)
    j = pl.program_id(1)
    i = pl.program_id(2)
    live = r < nr_ref[0]
    dot = lambda a, b: jnp.dot(a, b, preferred_element_type=F32)

    @pl.when(live & (i == 0) & (j < MOE_NF))
    def _():
        wgb_ref[...] = wg_ref[...].astype(BF16)
        wub_ref[...] = wu_ref[...].astype(BF16)

    @pl.when(live & (i == 0) & (j >= MOE_NF))
    def _():
        wob_ref[...] = wo_ref[...].astype(BF16)

    @pl.when(live & (i < nb_ref[r]))
    def _():
        nsub = (cnt_ref[b0_ref[r] + i] + (MOE_SUB - 1)) // MOE_SUB
        base = i * MOE_BM

        @pl.when(j == 0)
        def _():
            def cast(s, _):
                r0 = pl.multiple_of(s * MOE_SUB, MOE_SUB)
                rb = pl.multiple_of(base + s * MOE_SUB, MOE_SUB)
                xb_ref[pl.ds(rb, MOE_SUB), :] = x_ref[pl.ds(r0, MOE_SUB), :].astype(BF16)
                return 0

            lax.fori_loop(0, nsub, cast, 0)

        @pl.when(j < MOE_NF)
        def _():
            def sub(s, _):
                rb = pl.multiple_of(base + s * MOE_SUB, MOE_SUB)
                x = xb_ref[pl.ds(rb, MOE_SUB), :]
                g = jnp.minimum(dot(x, wgb_ref[...]) + bg_ref[...], LIMIT)
                up = jnp.clip(dot(x, wub_ref[...]) + bu_ref[...], -LIMIT, LIMIT)
                act = (up + 1.0) * (g * jax.nn.sigmoid(ALPHA * g))
                act_ref[jnp.minimum(j, MOE_NF - 1), pl.ds(rb, MOE_SUB), :] = act.astype(BF16)
                return 0

            lax.fori_loop(0, nsub, sub, 0)

        @pl.when(j >= MOE_NF)
        def _():
            def sub(s, _):
                r0 = pl.multiple_of(s * MOE_SUB, MOE_SUB)
                rb = pl.multiple_of(base + s * MOE_SUB, MOE_SUB)
                y = bo_ref[...] + dot(act_ref[0, pl.ds(rb, MOE_SUB), :], wob_ref[0:MOE_TF, :])
                for f in range(1, MOE_NF):
                    y = y + dot(act_ref[f, pl.ds(rb, MOE_SUB), :],
                                wob_ref[f * MOE_TF:(f + 1) * MOE_TF, :])
                o_ref[pl.ds(r0, MOE_SUB), :] = y
                return 0

            lax.fori_loop(0, nsub, sub, 0)

            def zero(s, _):
                r0 = pl.multiple_of(s * MOE_SUB, MOE_SUB)
                o_ref[pl.ds(r0, MOE_SUB), :] = jnp.zeros((MOE_SUB, MOE_TN), F32)
                return 0

            lax.fori_loop(nsub, MOE_BM // MOE_SUB, zero, 0)


def moe_experts(xs, run_e, run_b0, run_nb, block_cnt, n_runs, w_in, b_in, w_out, b_out, l):
    def f1(r, j, nr):
        return jnp.where(r < nr[0], jnp.minimum(j, MOE_NF - 1), MOE_NF - 1)

    def n1(r, j, nr):
        return jnp.where(r < nr[0], jnp.clip(j - MOE_NF, 0, MOE_NN - 1), MOE_NN - 1)

    def ii(r, i, nb):
        return jnp.minimum(i, nb[r] - 1)

    def x_map(r, j, i, re, b0, nb, cnt, nr):
        return (b0[r] + jnp.where(j == 0, ii(r, i, nb), nb[r] - 1), 0)

    def o_map(r, j, i, re, b0, nb, cnt, nr):
        return (b0[r] + jnp.where(j >= MOE_NF, ii(r, i, nb), 0), n1(r, j, nr))

    grid_spec = pltpu.PrefetchScalarGridSpec(
        num_scalar_prefetch=5,
        grid=(MOE_RUNS, MOE_NF + MOE_NN, MOE_RUN),
        in_specs=[
            pl.BlockSpec((MOE_BM, D), x_map),
            pl.BlockSpec((None, None, D, MOE_TF),
                         lambda r, j, i, re, b0, nb, cnt, nr: (l, re[r], 0, f1(r, j, nr))),
            pl.BlockSpec((None, None, D, MOE_TF),
                         lambda r, j, i, re, b0, nb, cnt, nr: (l, re[r], 0, MOE_NF + f1(r, j, nr))),
            pl.BlockSpec((None, None, 1, MOE_TF),
                         lambda r, j, i, re, b0, nb, cnt, nr: (l, re[r], 0, f1(r, j, nr))),
            pl.BlockSpec((None, None, 1, MOE_TF),
                         lambda r, j, i, re, b0, nb, cnt, nr: (l, re[r], 0, MOE_NF + f1(r, j, nr))),
            pl.BlockSpec((None, None, FF, MOE_TN),
                         lambda r, j, i, re, b0, nb, cnt, nr: (l, re[r], 0, n1(r, j, nr))),
            pl.BlockSpec((None, None, 1, MOE_TN),
                         lambda r, j, i, re, b0, nb, cnt, nr: (l, re[r], 0, n1(r, j, nr))),
        ],
        out_specs=pl.BlockSpec((MOE_BM, MOE_TN), o_map),
        scratch_shapes=[pltpu.VMEM((MOE_RUN * MOE_BM, D), BF16),
                        pltpu.VMEM((MOE_NF, MOE_RUN * MOE_BM, MOE_TF), BF16),
                        pltpu.VMEM((D, MOE_TF), BF16), pltpu.VMEM((D, MOE_TF), BF16),
                        pltpu.VMEM((FF, MOE_TN), BF16)],
    )
    return pl.pallas_call(
        _moe_kernel,
        grid_spec=grid_spec,
        out_shape=jax.ShapeDtypeStruct((MOE_ROWS, D), F32),
        compiler_params=_cp("arbitrary", "arbitrary", "arbitrary", vmem=60 << 20),
        name="moe_experts",
    )(run_e, run_b0, run_nb, block_cnt, n_runs, xs, w_in, w_in,
      b_in.reshape(DEPTH, N_EXP, 1, 2 * FF), b_in.reshape(DEPTH, N_EXP, 1, 2 * FF),
      w_out, b_out.reshape(DEPTH, N_EXP, 1, D))


def _combine_kernel(x_ref, y0_ref, y1_ref, y2_ref, y3_ref, gate_ref, mod_ref, o_ref):
    gate = gate_ref[...]
    acc = gate[:, 0:1] * y0_ref[...]
    for k, y_ref in enumerate((y1_ref, y2_ref, y3_ref), start=1):
        acc = acc + gate[:, k:k + 1] * y_ref[...]
    o_ref[...] = x_ref[...] + mod_ref[5:6, :] * acc


def moe_combine(x, yg, gate, mod, l):
    tm = 256
    nm = T // tm
    yspec = lambda k: pl.BlockSpec((tm, D), lambda m: (k * nm + m, 0))
    return pl.pallas_call(
        _combine_kernel,
        grid=(nm,),
        in_specs=[pl.BlockSpec((tm, D), lambda m: (m, 0)),
                  yspec(0), yspec(1), yspec(2), yspec(3),
                  pl.BlockSpec((tm, TOP_K), lambda m: (m, 0)),
                  pl.BlockSpec((None, None, 6, D), lambda m: (l, _grp_of_row(m * tm), 0, 0))],
        out_specs=pl.BlockSpec((tm, D), lambda m: (m, 0)),
        out_shape=jax.ShapeDtypeStruct((T, D), F32),
        compiler_params=_cp("parallel"),
        name="moe_combine",
    )(x, yg, yg, yg, yg, gate, mod)


def moe_routing(top_idx):
    n_assign = T * TOP_K
    flat_e = top_idx.reshape(-1)
    onehot = (flat_e[:, None] == jnp.arange(N_EXP, dtype=jnp.int32)[None, :]).astype(jnp.int32)
    csum = jnp.cumsum(onehot, axis=0)
    rank = jnp.take_along_axis(csum, flat_e[:, None], axis=1)[:, 0] - 1
    counts = csum[-1]
    nblk = (counts + MOE_BM - 1) // MOE_BM
    blk_end = jnp.cumsum(nblk)
    blk_start = blk_end - nblk
    n_used = blk_end[-1]
    dest = blk_start[flat_e] * MOE_BM + rank
    row_tok = jnp.zeros((MOE_ROWS,), jnp.int32).at[dest].set(
        jnp.arange(n_assign, dtype=jnp.int32) // TOP_K)
    mids = jnp.minimum(jnp.arange(MOE_BLOCKS, dtype=jnp.int32), n_used - 1)
    block_e = jnp.minimum(jnp.searchsorted(blk_end, mids, side='right'), N_EXP - 1).astype(jnp.int32)
    block_cnt = jnp.clip(counts[block_e] - (mids - blk_start[block_e]) * MOE_BM, 0, MOE_BM)
    block_cnt = jnp.where(jnp.arange(MOE_BLOCKS) < n_used, block_cnt, 0).astype(jnp.int32)
    nrun = (nblk + MOE_RUN - 1) // MOE_RUN
    run_end = jnp.cumsum(nrun)
    run_start = run_end - nrun
    n_runs = run_end[-1]
    ridx = jnp.arange(MOE_RUNS, dtype=jnp.int32)
    rids = jnp.minimum(ridx, n_runs - 1)
    run_e = jnp.minimum(jnp.searchsorted(run_end, rids, side='right'), N_EXP - 1).astype(jnp.int32)
    first = (rids - run_start[run_e]) * MOE_RUN
    run_b0 = blk_start[run_e] + first
    run_nb = jnp.minimum(MOE_RUN, nblk[run_e] - first)
    used = ridx < n_runs
    run_b0 = jnp.where(used, run_b0, run_b0 + run_nb - 1).astype(jnp.int32)
    run_nb = jnp.where(used, run_nb, 1).astype(jnp.int32)
    return row_tok, dest, run_e, run_b0, run_nb, block_cnt, n_runs.reshape(1).astype(jnp.int32)


def trunk_layer(x, l, mod, p, ssm_p, rope, cache_k, cache_v, h0):
    lambda_init = 0.8 - 0.6 * math.exp(-0.3 * l)
    zero_b = lambda n: jnp.zeros((DEPTH, 1, n), F32)

    h = norm_mod(x, p['norm_mix'], mod, l)
    z = matmul(h, p['w_in'], zero_b(IN_W), l, name="w_in")

    att = jnp.concatenate(
        [attn_ctx(z, p['diff_lambda'], p['diff_subln'], l, lambda_init),
         attn_lat(z, cache_k, cache_v, rope[0], rope[1], p['diff_lambda'], p['diff_subln'],
                  l, lambda_init)], axis=0)
    att_out = matmul(att, p['w_attn_out'], zero_b(D), l, out_dtype=BF16, name="attn_out")

    ab, bd, cd = ssm_p
    u_tm = jnp.transpose(z.reshape(N_BLK, BLK, IN_W)[:, :, 3 * QKV_W:3 * QKV_W + SSM_W], (1, 0, 2))
    y_tm, fin = ssm_mixer(u_tm, bd, cd, ab, p['ssm_d'].reshape(DEPTH, N_CC, 1, U_CH), h0, l)
    y = jnp.transpose(y_tm, (1, 0, 2)).reshape(T, SSM_W).astype(BF16)
    ssm_out = matmul_glu(y, p['w_ssm_glu'], l)

    cnv = conv_module(z, p['conv_w'], p['conv_b'], p['conv_ln_g'], p['conv_ln_b'], l)
    conv_out = matmul(cnv, p['w_conv_out'], p['b_conv_out'].reshape(DEPTH, 1, D), l,
                      out_dtype=BF16, name="conv_out")

    mixed = merge_gates(h, p['w_merge_gate'], p['b_merge_gate'].reshape(DEPTH, 1, 3 * D),
                        att_out, ssm_out, conv_out, l)
    x = matmul_resid(mixed, p['w_out'], x, mod, l, gate_row=2)

    h2p, top_idx, gate = norm_router(x, p['norm_ffn'], mod, p['router_w'], p['router_b'], l)
    row_tok, dest, run_e, run_b0, run_nb, block_cnt, n_runs = moe_routing(top_idx)
    xs = h2p.at[row_tok].get(mode='promise_in_bounds')
    ys = moe_experts(xs, run_e, run_b0, run_nb, block_cnt, n_runs,
                     p['moe_w_in'], p['moe_b_in'], p['moe_w_out'], p['moe_b_out'], l)
    dest_k = dest.reshape(T, TOP_K).T.reshape(-1)
    yg = ys.at[dest_k].get(mode='promise_in_bounds')
    x = moe_combine(x, yg, gate, mod, l)

    k_new = z[:T_CTX, QKV_W:2 * QKV_W].reshape(BATCH, SEQ, 2 * HEADS, HEAD_DIM)
    v_new = z[:T_CTX, 2 * QKV_W:3 * QKV_W].reshape(BATCH, SEQ, HEADS, 2 * HEAD_DIM)
    s_new = jnp.transpose(fin[:, :, :BATCH, :], (2, 0, 1, 3)).reshape(BATCH, 2, 2, SSM_G, SSM_N)
    return x, k_new, v_new, s_new


def kernel(x_prompt, x_sample, cache_k, cache_v, state_ssm, c, c_ctx, w_ada, b_ada, norm_mix, norm_ffn, w_in, diff_lambda, diff_subln, w_attn_out, ssm_a_re, ssm_a_im, ssm_log_dt, ssm_b, ssm_c, ssm_d, w_ssm_glu, conv_w, conv_b, conv_ln_g, conv_ln_b, w_conv_out, b_conv_out, w_merge_gate, b_merge_gate, w_out, router_w, router_b, moe_w_in, moe_b_in, moe_w_out, moe_b_out, norm_final):
    p = dict(norm_mix=norm_mix, norm_ffn=norm_ffn, w_in=w_in, diff_lambda=diff_lambda,
             diff_subln=diff_subln, w_attn_out=w_attn_out, ssm_d=ssm_d, w_ssm_glu=w_ssm_glu,
             conv_w=conv_w, conv_b=conv_b, conv_ln_g=conv_ln_g, conv_ln_b=conv_ln_b,
             w_conv_out=w_conv_out, b_conv_out=b_conv_out, w_merge_gate=w_merge_gate,
             b_merge_gate=b_merge_gate, w_out=w_out, router_w=router_w, router_b=router_b,
             moe_w_in=moe_w_in, moe_b_in=moe_b_in, moe_w_out=moe_w_out, moe_b_out=moe_b_out)

    cond8 = jnp.zeros((SUB, D), F32).at[0].set(c_ctx).at[1:1 + DEC_BATCH].set(c)
    mod = ada_all(cond8, w_ada, b_ada).reshape(DEPTH, SUB, 6, D)

    ab_re, ab_im, bb_re, bb_im = ssm_discretize(ssm_a_re, ssm_a_im, ssm_log_dt, ssm_b)
    ssm_p = ssm_weights(ab_re, ab_im, bb_re, bb_im, ssm_c)
    rope = rope_tables()

    ck = cache_k.reshape(DEC_BATCH, DEPTH, PAST, QKV_W)
    cv = cache_v.reshape(DEC_BATCH, DEPTH, PAST, QKV_W)
    st = state_ssm.reshape(DEC_BATCH, DEPTH, 2, 2, 1, SSM_G * SSM_N)
    h0_all = jnp.concatenate([jnp.zeros((N_SG - DEC_BATCH,) + st.shape[1:], F32), st], axis=0)

    x = jnp.concatenate([x_prompt.reshape(T_CTX, D), x_sample.reshape(T_LAT, D)], axis=0)
    new_k, new_v, new_s = [], [], []
    for l in range(DEPTH):
        x, k_l, v_l, s_l = trunk_layer(x, l, mod, p, ssm_p, rope, ck, cv, h0_all[:, l])
        new_k.append(k_l)
        new_v.append(v_l)
        new_s.append(s_l)
    y = final_norm(x, norm_final)
    return (y[:T_CTX].reshape(BATCH, SEQ, D), y[T_CTX:].reshape(DEC_BATCH, DEC_SEQ, D),
            jnp.stack(new_k, axis=1), jnp.stack(new_v, axis=1), jnp.stack(new_s, axis=1))
```

```python
import functools
import math

import jax
import jax.numpy as jnp
from jax import lax
from jax.experimental import pallas as pl
from jax.experimental.pallas import tpu as pltpu

F32 = jnp.float32
BF16 = jnp.bfloat16

D = 2048
BATCH = 32
SEQ = 256
DEPTH = 4
DEC_BATCH = 2
DEC_SEQ = 2048
PAST = 512
GRID_W = 64
HEADS = 8
HEAD_DIM = 64
ROPE_PAIRS = 16
ROPE_BASE = 10000.0
QKV_W = 1024
SSM_W = 1024
SSM_G = 64
SSM_P = 16
SSM_N = 64
CONV_CH = 1024
CONV_K = 31
IN_W = 6144
N_EXP = 32
TOP_K = 4
FF = 2048
LIMIT = 7.0
ALPHA = 1.702
EPS = 1e-6

T_CTX = BATCH * SEQ
T_LAT = DEC_BATCH * DEC_SEQ
T = T_CTX + T_LAT
BLK = 256
N_BLK = T // BLK
CTX_BLKS = T_CTX // BLK
LAT_BLKS = DEC_SEQ // BLK
N_GRP = 1 + DEC_BATCH
SUB = 8
LANE = 128
STATE_CH = 1024
U_CH = 256
N_CC = SSM_W // U_CH
N_SG = N_BLK // SUB

MOE_BM = 1024
MOE_SUB = 256
MOE_TF = 512
MOE_NF = FF // MOE_TF
MOE_TN = 512
MOE_NN = D // MOE_TN
MOE_RUN = 1
MOE_RUNS = (T * TOP_K) // (MOE_RUN * MOE_BM) + N_EXP
MOE_BLOCKS = (T * TOP_K) // MOE_BM + N_EXP
MOE_ROWS = MOE_BLOCKS * MOE_BM

VMEM_LIMIT = 56 << 20


def _cp(*sem, vmem=VMEM_LIMIT):
    return pltpu.CompilerParams(dimension_semantics=sem, vmem_limit_bytes=vmem)


def _grp_of_row(row):
    return jnp.where(row < T_CTX, 0, 1 + (row - T_CTX) // DEC_SEQ)


def _ada_kernel(c_ref, w_ref, b_ref, o_ref):
    c = c_ref[...]
    a = (c * jax.nn.sigmoid(c)).astype(BF16)
    o_ref[...] = jnp.dot(a, w_ref[...].astype(BF16), preferred_element_type=F32) + b_ref[...]


def ada_all(cond8, w_ada, b_ada):
    tn = 1024
    n = 6 * D
    return pl.pallas_call(
        _ada_kernel,
        grid=(DEPTH, n // tn),
        in_specs=[pl.BlockSpec((SUB, D), lambda l, j: (0, 0)),
                  pl.BlockSpec((None, D, tn), lambda l, j: (l, 0, j)),
                  pl.BlockSpec((None, 1, tn), lambda l, j: (l, 0, j))],
        out_specs=pl.BlockSpec((None, SUB, tn), lambda l, j: (l, 0, j)),
        out_shape=jax.ShapeDtypeStruct((DEPTH, SUB, n), F32),
        compiler_params=_cp("parallel", "parallel"),
        name="ada",
    )(cond8, w_ada, b_ada.reshape(DEPTH, 1, n))


def _rms(x, g):
    return x * lax.rsqrt(jnp.mean(x * x, axis=-1, keepdims=True) + EPS) * g


def _norm_mod_kernel(x_ref, g_ref, mod_ref, o_ref):
    y = _rms(x_ref[...], g_ref[...])
    sh = mod_ref[0:1, :]
    sc = mod_ref[1:2, :]
    o_ref[...] = (y * (1.0 + sc) + sh).astype(o_ref.dtype)


def norm_mod(x, g, mod, l):
    tm = 512
    return pl.pallas_call(
        _norm_mod_kernel,
        grid=(T // tm,),
        in_specs=[pl.BlockSpec((tm, D), lambda m: (m, 0)),
                  pl.BlockSpec((None, 1, D), lambda m: (l, 0, 0)),
                  pl.BlockSpec((None, None, 6, D), lambda m: (l, _grp_of_row(m * tm), 0, 0))],
        out_specs=pl.BlockSpec((tm, D), lambda m: (m, 0)),
        out_shape=jax.ShapeDtypeStruct((T, D), BF16),
        compiler_params=_cp("parallel"),
        name="norm_mod",
    )(x, g.reshape(DEPTH, 1, D), mod)


def _final_norm_kernel(x_ref, g_ref, o_ref):
    o_ref[...] = _rms(x_ref[...], g_ref[...])


def final_norm(x, g):
    tm = 512
    return pl.pallas_call(
        _final_norm_kernel,
        grid=(T // tm,),
        in_specs=[pl.BlockSpec((tm, D), lambda m: (m, 0)),
                  pl.BlockSpec((1, D), lambda m: (0, 0))],
        out_specs=pl.BlockSpec((tm, D), lambda m: (m, 0)),
        out_shape=jax.ShapeDtypeStruct((T, D), F32),
        compiler_params=_cp("parallel"),
        name="final_norm",
    )(x, g.reshape(1, D))


def _mm_kernel(a_ref, w_ref, b_ref, o_ref, wb_ref):
    @pl.when(pl.program_id(1) == 0)
    def _():
        wb_ref[...] = w_ref[...].astype(BF16)

    acc = jnp.dot(a_ref[...], wb_ref[...], preferred_element_type=F32)
    o_ref[...] = (acc + b_ref[...]).astype(o_ref.dtype)


def matmul(a, w, bias, l, *, tm=1024, tn=1024, out_dtype=F32, name="mm"):
    k, n = w.shape[1], w.shape[2]
    return pl.pallas_call(
        _mm_kernel,
        grid=(n // tn, T // tm),
        in_specs=[pl.BlockSpec((tm, k), lambda j, m: (m, 0)),
                  pl.BlockSpec((None, k, tn), lambda j, m: (l, 0, j)),
                  pl.BlockSpec((None, 1, tn), lambda j, m: (l, 0, j))],
        out_specs=pl.BlockSpec((tm, tn), lambda j, m: (m, j)),
        out_shape=jax.ShapeDtypeStruct((T, n), out_dtype),
        scratch_shapes=[pltpu.VMEM((k, tn), BF16)],
        compiler_params=_cp("parallel", "arbitrary"),
        name=name,
    )(a, w, bias)


def _mm_glu_kernel(a_ref, wa_ref, wg_ref, o_ref, wab_ref, wgb_ref):
    @pl.when(pl.program_id(1) == 0)
    def _():
        wab_ref[...] = wa_ref[...].astype(BF16)
        wgb_ref[...] = wg_ref[...].astype(BF16)

    a = a_ref[...]
    va = jnp.dot(a, wab_ref[...], preferred_element_type=F32)
    vg = jnp.dot(a, wgb_ref[...], preferred_element_type=F32)
    o_ref[...] = (va * jax.nn.sigmoid(vg)).astype(o_ref.dtype)


def matmul_glu(a, w, l, *, tm=1024, tn=512):
    k, n = w.shape[1], w.shape[2] // 2
    nj = n // tn
    return pl.pallas_call(
        _mm_glu_kernel,
        grid=(nj, T // tm),
        in_specs=[pl.BlockSpec((tm, k), lambda j, m: (m, 0)),
                  pl.BlockSpec((None, k, tn), lambda j, m: (l, 0, j)),
                  pl.BlockSpec((None, k, tn), lambda j, m: (l, 0, nj + j))],
        out_specs=pl.BlockSpec((tm, tn), lambda j, m: (m, j)),
        out_shape=jax.ShapeDtypeStruct((T, n), BF16),
        scratch_shapes=[pltpu.VMEM((k, tn), BF16), pltpu.VMEM((k, tn), BF16)],
        compiler_params=_cp("parallel", "arbitrary"),
        name="ssm_glu",
    )(a, w, w)


def _mm_resid_kernel(a_ref, w_ref, x_ref, mod_ref, o_ref, wb_ref, *, gate_row):
    @pl.when(pl.program_id(1) == 0)
    def _():
        wb_ref[...] = w_ref[...].astype(BF16)

    acc = jnp.dot(a_ref[...], wb_ref[...], preferred_element_type=F32)
    o_ref[...] = x_ref[...] + mod_ref[gate_row:gate_row + 1, :] * acc


def matmul_resid(a, w, x, mod, l, *, gate_row, tm=1024, tn=1024):
    k, n = w.shape[1], w.shape[2]
    return pl.pallas_call(
        functools.partial(_mm_resid_kernel, gate_row=gate_row),
        grid=(n // tn, T // tm),
        in_specs=[pl.BlockSpec((tm, k), lambda j, m: (m, 0)),
                  pl.BlockSpec((None, k, tn), lambda j, m: (l, 0, j)),
                  pl.BlockSpec((tm, tn), lambda j, m: (m, j)),
                  pl.BlockSpec((None, None, 6, tn), lambda j, m: (l, _grp_of_row(m * tm), 0, j))],
        out_specs=pl.BlockSpec((tm, tn), lambda j, m: (m, j)),
        out_shape=jax.ShapeDtypeStruct((T, n), F32),
        scratch_shapes=[pltpu.VMEM((k, tn), BF16)],
        compiler_params=_cp("parallel", "arbitrary"),
        name="out_resid",
    )(a, w, x, mod)


def _merge_kernel(h_ref, w0_ref, w1_ref, w2_ref, b0_ref, b1_ref, b2_ref,
                  att_ref, ssm_ref, cnv_ref, o_ref, wb0_ref, wb1_ref, wb2_ref):
    @pl.when(pl.program_id(1) == 0)
    def _():
        wb0_ref[...] = w0_ref[...].astype(BF16)
        wb1_ref[...] = w1_ref[...].astype(BF16)
        wb2_ref[...] = w2_ref[...].astype(BF16)

    h = h_ref[...]

    def gate(wb_ref, b_ref):
        return jax.nn.sigmoid(jnp.dot(h, wb_ref[...], preferred_element_type=F32) + b_ref[...])

    mixed = (gate(wb0_ref, b0_ref) * att_ref[...] + gate(wb1_ref, b1_ref) * ssm_ref[...]
             + gate(wb2_ref, b2_ref) * cnv_ref[...])
    o_ref[...] = mixed.astype(o_ref.dtype)


def merge_gates(h, w, b, att, ssm, cnv, l, *, tm=512, tn=512):
    nj = D // tn
    wspec = lambda br: pl.BlockSpec((None, D, tn), lambda j, m: (l, 0, br * nj + j))
    bspec = lambda br: pl.BlockSpec((None, 1, tn), lambda j, m: (l, 0, br * nj + j))
    tile = pl.BlockSpec((tm, tn), lambda j, m: (m, j))
    return pl.pallas_call(
        _merge_kernel,
        grid=(nj, T // tm),
        in_specs=[pl.BlockSpec((tm, D), lambda j, m: (m, 0)),
                  wspec(0), wspec(1), wspec(2), bspec(0), bspec(1), bspec(2),
                  tile, tile, tile],
        out_specs=tile,
        out_shape=jax.ShapeDtypeStruct((T, D), BF16),
        scratch_shapes=[pltpu.VMEM((D, tn), BF16)] * 3,
        compiler_params=_cp("parallel", "arbitrary"),
        name="merge_gates",
    )(h, w, w, w, b, b, b, att, ssm, cnv)


def _lambda(lam_ref, lambda_init):
    lp = lam_ref[...]
    s01 = jnp.sum(lp[0:1, :] * lp[1:2, :], axis=-1, keepdims=True)
    s23 = jnp.sum(lp[2:3, :] * lp[3:4, :], axis=-1, keepdims=True)
    return jnp.exp(s01) - jnp.exp(s23) + lambda_init


def _softmax(s):
    e = jnp.exp(s - jnp.max(s, axis=-1, keepdims=True))
    return e * (1.0 / jnp.sum(e, axis=-1, keepdims=True))


def _diff_attn_head(q, kb, vb, lam, g, lambda_init):
    lane = lax.broadcasted_iota(jnp.int32, q.shape, 1)
    q_a = jnp.where(lane < HEAD_DIM, q, 0.0).astype(BF16)
    q_b = jnp.where(lane >= HEAD_DIM, q, 0.0).astype(BF16)
    nt = (((1,), (1,)), ((), ()))
    p_a = _softmax(lax.dot_general(q_a, kb, nt, preferred_element_type=F32))
    p_b = _softmax(lax.dot_general(q_b, kb, nt, preferred_element_type=F32))
    a = (p_a - lam * p_b).astype(BF16)
    o = jnp.dot(a, vb, preferred_element_type=F32)
    return _rms(o, g) * (1.0 - lambda_init)


def _attn_ctx_kernel(q_ref, k_ref, v_ref, lam_ref, g_ref, o_ref, *, lambda_init):
    lam = _lambda(lam_ref, lambda_init)
    g = g_ref[...]
    for h in range(HEADS):
        cols = slice(h * LANE, (h + 1) * LANE)
        q = q_ref[:, cols] * (HEAD_DIM ** -0.5)
        o = _diff_attn_head(q, k_ref[:, cols].astype(BF16), v_ref[:, cols].astype(BF16),
                            lam, g, lambda_init)
        o_ref[:, cols] = o.astype(o_ref.dtype)


def attn_ctx(z, lam_p, subln, l, lambda_init):
    return pl.pallas_call(
        functools.partial(_attn_ctx_kernel, lambda_init=lambda_init),
        grid=(BATCH,),
        in_specs=[pl.BlockSpec((SEQ, QKV_W), lambda b: (b, 0)),
                  pl.BlockSpec((SEQ, QKV_W), lambda b: (b, 1)),
                  pl.BlockSpec((SEQ, QKV_W), lambda b: (b, 2)),
                  pl.BlockSpec((None, 4, HEAD_DIM), lambda b: (l, 0, 0)),
                  pl.BlockSpec((None, 1, LANE), lambda b: (l, 0, 0))],
        out_specs=pl.BlockSpec((SEQ, QKV_W), lambda b: (b, 0)),
        out_shape=jax.ShapeDtypeStruct((T_CTX, QKV_W), BF16),
        compiler_params=_cp("parallel"),
        name="attn_ctx",
    )(z, z, z, lam_p, subln.reshape(DEPTH, 1, LANE))


def _rope(x, cos, sin_signed):
    lane = lax.broadcasted_iota(jnp.int32, x.shape, 1)
    first = (lane % (2 * ROPE_PAIRS)) < ROPE_PAIRS
    partner = jnp.where(first, pltpu.roll(x, LANE - ROPE_PAIRS, 1), pltpu.roll(x, ROPE_PAIRS, 1))
    return x * cos + partner * sin_signed


def _attn_lat_kernel(q_ref, k_ref, v_ref, ck_ref, cv_ref, cq_ref, sq_ref, ckk_ref, skk_ref,
                     lam_ref, g_ref, o_ref, kb_ref, vb_ref, *, lambda_init):
    @pl.when(pl.program_id(2) == 0)
    def _():
        kb_ref[0:DEC_SEQ, :] = _rope(k_ref[...], ckk_ref[...], skk_ref[...]).astype(BF16)
        kb_ref[DEC_SEQ:DEC_SEQ + PAST, :] = ck_ref[...].astype(BF16)
        vb_ref[0:DEC_SEQ, :] = v_ref[...].astype(BF16)
        vb_ref[DEC_SEQ:DEC_SEQ + PAST, :] = cv_ref[...].astype(BF16)

    q = _rope(q_ref[...], cq_ref[...], sq_ref[...]) * (HEAD_DIM ** -0.5)
    o = _diff_attn_head(q, kb_ref[...], vb_ref[...], _lambda(lam_ref, lambda_init), g_ref[...],
                        lambda_init)
    o_ref[...] = o.astype(o_ref.dtype)


def attn_lat(z, cache_k, cache_v, cos_t, sin_t, lam_p, subln, l, lambda_init):
    tq = BLK
    nq = DEC_SEQ // tq
    qrow = lambda b, h, i: CTX_BLKS + b * nq + i
    krow = lambda b: T_CTX // DEC_SEQ + b
    return pl.pallas_call(
        functools.partial(_attn_lat_kernel, lambda_init=lambda_init),
        grid=(DEC_BATCH, HEADS, nq),
        in_specs=[pl.BlockSpec((tq, LANE), lambda b, h, i: (qrow(b, h, i), h)),
                  pl.BlockSpec((DEC_SEQ, LANE), lambda b, h, i: (krow(b), HEADS + h)),
                  pl.BlockSpec((DEC_SEQ, LANE), lambda b, h, i: (krow(b), 2 * HEADS + h)),
                  pl.BlockSpec((None, None, PAST, LANE), lambda b, h, i: (b, l, 0, h)),
                  pl.BlockSpec((None, None, PAST, LANE), lambda b, h, i: (b, l, 0, h)),
                  pl.BlockSpec((tq, LANE), lambda b, h, i: (i, 0)),
                  pl.BlockSpec((tq, LANE), lambda b, h, i: (i, 0)),
                  pl.BlockSpec((DEC_SEQ, LANE), lambda b, h, i: (0, 0)),
                  pl.BlockSpec((DEC_SEQ, LANE), lambda b, h, i: (0, 0)),
                  pl.BlockSpec((None, 4, HEAD_DIM), lambda b, h, i: (l, 0, 0)),
                  pl.BlockSpec((None, 1, LANE), lambda b, h, i: (l, 0, 0))],
        out_specs=pl.BlockSpec((tq, LANE), lambda b, h, i: (b * nq + i, h)),
        out_shape=jax.ShapeDtypeStruct((T_LAT, QKV_W), BF16),
        scratch_shapes=[pltpu.VMEM((DEC_SEQ + PAST, LANE), BF16)] * 2,
        compiler_params=_cp("parallel", "parallel", "arbitrary"),
        name="attn_lat",
    )(z, z, z, cache_k, cache_v, cos_t, sin_t, cos_t, sin_t, lam_p,
      subln.reshape(DEPTH, 1, LANE))


def rope_tables():
    row = (jnp.arange(DEC_SEQ) // GRID_W).astype(F32)
    col = (jnp.arange(DEC_SEQ) % GRID_W).astype(F32)
    inv_freq = ROPE_BASE ** (-jnp.arange(ROPE_PAIRS, dtype=F32) / ROPE_PAIRS)
    ang = jnp.stack([row[:, None] * inv_freq, col[:, None] * inv_freq], axis=1)
    cos, sin = jnp.cos(ang), jnp.sin(ang)
    cos64 = jnp.concatenate([cos, cos], axis=-1).reshape(DEC_SEQ, HEAD_DIM)
    sin64 = jnp.concatenate([-sin, sin], axis=-1).reshape(DEC_SEQ, HEAD_DIM)
    return jnp.tile(cos64, (1, 2)), jnp.tile(sin64, (1, 2))


def _zoh_kernel(are_ref, aim_ref, ldt_ref, bre_ref, bim_ref,
                abr_ref, abi_ref, bbr_ref, bbi_ref):
    a_re = are_ref[...]
    a_im = aim_ref[...]
    dt = jnp.exp(ldt_ref[...])
    mag = jnp.exp(dt * a_re)
    ang = dt * a_im
    ab_re = mag * jnp.cos(ang)
    ab_im = mag * jnp.sin(ang)
    den = a_re * a_re + a_im * a_im
    xm = ab_re - 1.0
    f_re = (xm * a_re + ab_im * a_im) / den
    f_im = (ab_im * a_re - xm * a_im) / den
    abr_ref[...] = ab_re
    abi_ref[...] = ab_im
    rows = a_re.shape[0]
    fr = jnp.broadcast_to(f_re[:, None, :], (rows, SSM_P, SSM_N))
    fi = jnp.broadcast_to(f_im[:, None, :], (rows, SSM_P, SSM_N))
    b_re = bre_ref[...]
    b_im = bim_ref[...]
    bbr_ref[...] = fr * b_re - fi * b_im
    bbi_ref[...] = fr * b_im + fi * b_re


def ssm_discretize(a_re, a_im, log_dt, b):
    rows = DEPTH * 2 * SSM_G
    bt = jnp.swapaxes(b, -1, -2)
    b_re = bt[:, :, 0].reshape(rows, SSM_P, SSM_N)
    b_im = bt[:, :, 1].reshape(rows, SSM_P, SSM_N)
    sd = jax.ShapeDtypeStruct
    ab_re, ab_im, bb_re, bb_im = pl.pallas_call(
        _zoh_kernel,
        out_shape=[sd((rows, SSM_N), F32), sd((rows, SSM_N), F32),
                   sd((rows, SSM_P, SSM_N), F32), sd((rows, SSM_P, SSM_N), F32)],
        compiler_params=pltpu.CompilerParams(vmem_limit_bytes=VMEM_LIMIT),
        name="ssm_zoh",
    )(a_re.reshape(rows, SSM_N), a_im.reshape(rows, SSM_N), log_dt.reshape(rows, 1), b_re, b_im)
    return ab_re, ab_im, bb_re, bb_im


def _block_diag(m):
    eye = jnp.eye(16, dtype=m.dtype)
    out = m[..., :, :, None, :] * eye[:, None, :, None]
    return out.reshape(*m.shape[:-3], 16 * m.shape[-2], 16 * m.shape[-1])


def ssm_weights(ab_re, ab_im, bb_re, bb_im, c):
    ab = jnp.stack([ab_re, ab_im], axis=0).reshape(2, DEPTH, 2, N_CC, 1, STATE_CH)
    ab = jnp.transpose(ab, (1, 2, 0, 3, 4, 5))
    bb = jnp.stack([bb_re, bb_im], axis=0).reshape(2, DEPTH, 2, N_CC, 16, SSM_P, SSM_N)
    bd = _block_diag(jnp.transpose(bb, (1, 2, 0, 3, 4, 5, 6))).astype(BF16)
    ct = jnp.swapaxes(c, -1, -2).reshape(DEPTH, 2, 2, N_CC, 16, SSM_N, SSM_P)
    cd = _block_diag(ct).astype(BF16)
    return ab, bd, cd


def _cmul_add(ar, ai, xr, xi, br, bi):
    return ar * xr - ai * xi + br, ar * xi + ai * xr + bi


def _ssm_kernel(u_ref, bd_ref, cd_ref, ab_ref, d_ref, h0_ref, y_ref, fin_ref,
                xr_ref, xi_ref, tr_ref, ti_ref, acc_ref, cin_ref):
    nj = BLK
    is_lat = pl.program_id(0) >= CTX_BLKS // SUB
    u = u_ref[...].reshape(nj * SUB, U_CH)
    ub = u.astype(BF16)
    acc_ref[...] = u * d_ref[...]
    for d in range(2):
        xr_ref[...] = jnp.dot(ub, bd_ref[d, 0], preferred_element_type=F32)
        xi_ref[...] = jnp.dot(ub, bd_ref[d, 1], preferred_element_type=F32)
        a1r = ab_ref[d, 0]
        a1i = ab_ref[d, 1]
        ar = jnp.broadcast_to(a1r, (SUB, STATE_CH))
        ai = jnp.broadcast_to(a1i, (SUB, STATE_CH))

        def step(i, carry, d=d, ar=ar, ai=ai):
            j = i if d == 0 else nj - 1 - i
            r0 = pl.multiple_of(j * SUB, SUB)
            nr, ni = _cmul_add(ar, ai, carry[0], carry[1],
                               xr_ref[pl.ds(r0, SUB), :], xi_ref[pl.ds(r0, SUB), :])
            xr_ref[pl.ds(r0, SUB), :] = nr
            xi_ref[pl.ds(r0, SUB), :] = ni
            return nr, ni

        zero = jnp.zeros((SUB, STATE_CH), F32)
        lax.fori_loop(0, nj, step, (zero, zero), unroll=4)

        last = (nj - 1) * SUB if d == 0 else 0
        fin_ref[d, 0] = xr_ref[last:last + SUB, :]
        fin_ref[d, 1] = xi_ref[last:last + SUB, :]

        @pl.when(is_lat)
        def _(d=d, a1r=a1r, a1i=a1i, last=last):
            def pstep(j, p):
                tr_ref[pl.ds(j, 1), :] = p[0]
                ti_ref[pl.ds(j, 1), :] = p[1]
                return _cmul_add(a1r, a1i, p[0], p[1], 0.0, 0.0)

            lax.fori_loop(0, nj, pstep, (a1r, a1i))
            anr = tr_ref[nj - 1:nj, :]
            ani = ti_ref[nj - 1:nj, :]
            cr = [None] * SUB
            ci = [None] * SUB
            order = list(range(SUB)) if d == 0 else list(range(SUB - 1, -1, -1))
            cr[order[0]] = h0_ref[d, 0]
            ci[order[0]] = h0_ref[d, 1]
            for prev, cur in zip(order[:-1], order[1:]):
                fr = xr_ref[last + prev:last + prev + 1, :]
                fi = xi_ref[last + prev:last + prev + 1, :]
                cr[cur], ci[cur] = _cmul_add(anr, ani, cr[prev], ci[prev], fr, fi)
            for s in range(SUB):
                cin_ref[0, s:s + 1, :] = cr[s]
                cin_ref[1, s:s + 1, :] = ci[s]
            cin_r = cin_ref[0]
            cin_i = cin_ref[1]

            def fstep(i, _):
                t = i if d == 0 else nj - 1 - i
                r0 = pl.multiple_of(i * SUB, SUB)
                pr = tr_ref[pl.ds(t, 1), :]
                pi = ti_ref[pl.ds(t, 1), :]
                nr, ni = _cmul_add(pr, pi, cin_r, cin_i,
                                   xr_ref[pl.ds(r0, SUB), :], xi_ref[pl.ds(r0, SUB), :])
                xr_ref[pl.ds(r0, SUB), :] = nr
                xi_ref[pl.ds(r0, SUB), :] = ni
                return 0

            lax.fori_loop(0, nj, fstep, 0, unroll=2)

        acc_ref[...] += (jnp.dot(xr_ref[...].astype(BF16), cd_ref[d, 0], preferred_element_type=F32)
                         - jnp.dot(xi_ref[...].astype(BF16), cd_ref[d, 1], preferred_element_type=F32))

    y_ref[...] = jax.nn.gelu(acc_ref[...]).reshape(nj, SUB, U_CH)


def ssm_mixer(u_tm, bd, cd, ab, dvec, h0, l):
    nrow = BLK * SUB
    sd = jax.ShapeDtypeStruct
    return pl.pallas_call(
        _ssm_kernel,
        grid=(N_SG, N_CC),
        in_specs=[pl.BlockSpec((BLK, SUB, U_CH), lambda s, c: (0, s, c)),
                  pl.BlockSpec((None, 2, 2, None, U_CH, STATE_CH), lambda s, c: (l, 0, 0, c, 0, 0)),
                  pl.BlockSpec((None, 2, 2, None, STATE_CH, U_CH), lambda s, c: (l, 0, 0, c, 0, 0)),
                  pl.BlockSpec((None, 2, 2, None, 1, STATE_CH), lambda s, c: (l, 0, 0, c, 0, 0)),
                  pl.BlockSpec((None, None, 1, U_CH), lambda s, c: (l, c, 0, 0)),
                  pl.BlockSpec((None, 2, 2, 1, STATE_CH), lambda s, c: (s, 0, 0, 0, c))],
        out_specs=[pl.BlockSpec((BLK, SUB, U_CH), lambda s, c: (0, s, c)),
                   pl.BlockSpec((2, 2, SUB, STATE_CH), lambda s, c: (0, 0, s, c))],
        out_shape=[sd((BLK, N_BLK, SSM_W), F32), sd((2, 2, N_BLK, SSM_G * SSM_N), F32)],
        scratch_shapes=[pltpu.VMEM((nrow, STATE_CH), F32), pltpu.VMEM((nrow, STATE_CH), F32),
                        pltpu.VMEM((BLK, STATE_CH), F32), pltpu.VMEM((BLK, STATE_CH), F32),
                        pltpu.VMEM((nrow, U_CH), F32), pltpu.VMEM((2, SUB, STATE_CH), F32)],
        compiler_params=_cp("parallel", "parallel"),
        name="ssm",
    )(u_tm, bd, cd, ab, dvec, h0)


CONV_HALO = 16
CONV_RC = 32


def _conv_kernel(xa_ref, xg_ref, pa_ref, pg_ref, na_ref, ng_ref, w_ref, cb_ref, lg_ref, lb_ref,
                 o_ref, xp_ref, xs_ref):
    r = pl.program_id(0)
    is_lat = r >= CTX_BLKS
    pos = jnp.bitwise_and(r, LAT_BLKS - 1)
    pv = jnp.where(is_lat & (pos != 0), 1.0, 0.0)
    nv = jnp.where(is_lat & (pos != LAT_BLKS - 1), 1.0, 0.0)
    glu = lambda a_ref, g_ref: a_ref[...] * jax.nn.sigmoid(g_ref[...])
    xp_ref[0:CONV_HALO, :] = glu(pa_ref, pg_ref) * pv
    xp_ref[CONV_HALO:CONV_HALO + BLK, :] = glu(xa_ref, xg_ref)
    xp_ref[CONV_HALO + BLK:, :] = glu(na_ref, ng_ref) * nv
    span = BLK + 3 * SUB
    for s in range(SUB):
        xs_ref[s] = xp_ref[s:s + span, :]

    def chunk(c, _):
        r0 = pl.multiple_of(c * CONV_RC, CONV_RC)
        acc = jnp.zeros((CONV_RC, CONV_CH), F32)
        for k in range(CONV_K):
            off = k + CONV_HALO - CONV_K // 2
            acc = acc + xs_ref[off % SUB, pl.ds(r0 + (off // SUB) * SUB, CONV_RC), :] * w_ref[k:k + 1, :]
        acc = acc + cb_ref[...]
        mu = jnp.mean(acc, axis=-1, keepdims=True)
        xc = acc - mu
        var = jnp.mean(xc * xc, axis=-1, keepdims=True)
        yn = xc * lax.rsqrt(var + EPS) * lg_ref[...] + lb_ref[...]
        o_ref[pl.ds(r0, CONV_RC), :] = (yn * jax.nn.sigmoid(yn)).astype(o_ref.dtype)
        return 0

    lax.fori_loop(0, BLK // CONV_RC, chunk, 0)


def conv_module(z, conv_w, conv_b, ln_g, ln_b, l):
    hb = BLK // CONV_HALO
    last_halo = T // CONV_HALO - 1
    ca, cg = 4, 5
    prev = lambda r: jnp.maximum(r * hb - 1, 0)
    nxt = lambda r: jnp.minimum((r + 1) * hb, last_halo)
    vec = lambda: pl.BlockSpec((None, 1, CONV_CH), lambda r: (l, 0, 0))
    return pl.pallas_call(
        _conv_kernel,
        grid=(N_BLK,),
        in_specs=[pl.BlockSpec((BLK, CONV_CH), lambda r: (r, ca)),
                  pl.BlockSpec((BLK, CONV_CH), lambda r: (r, cg)),
                  pl.BlockSpec((CONV_HALO, CONV_CH), lambda r: (prev(r), ca)),
                  pl.BlockSpec((CONV_HALO, CONV_CH), lambda r: (prev(r), cg)),
                  pl.BlockSpec((CONV_HALO, CONV_CH), lambda r: (nxt(r), ca)),
                  pl.BlockSpec((CONV_HALO, CONV_CH), lambda r: (nxt(r), cg)),
                  pl.BlockSpec((None, CONV_K, CONV_CH), lambda r: (l, 0, 0)),
                  vec(), vec(), vec()],
        out_specs=pl.BlockSpec((BLK, CONV_CH), lambda r: (r, 0)),
        out_shape=jax.ShapeDtypeStruct((T, CONV_CH), BF16),
        scratch_shapes=[pltpu.VMEM((BLK + 2 * CONV_HALO, CONV_CH), F32),
                        pltpu.VMEM((SUB, BLK + 3 * SUB, CONV_CH), F32)],
        compiler_params=_cp("parallel"),
        name="conv",
    )(z, z, z, z, z, z, conv_w, conv_b.reshape(DEPTH, 1, CONV_CH),
      ln_g.reshape(DEPTH, 1, CONV_CH), ln_b.reshape(DEPTH, 1, CONV_CH))


def _norm_router_kernel(x_ref, g_ref, mod_ref, rw_ref, rb_ref, h_ref, idx_ref, gate_ref):
    y = _rms(x_ref[...], g_ref[...])
    h = y * (1.0 + mod_ref[4:5, :]) + mod_ref[3:4, :]
    h_ref[...] = h
    hb = h.astype(BF16)
    h_lo = (h - hb.astype(F32)).astype(BF16)
    w = rw_ref[...]
    wb = w.astype(BF16)
    w_lo = (w - wb.astype(F32)).astype(BF16)
    dot = lambda a, b: jnp.dot(a, b, preferred_element_type=F32)
    logits = dot(hb, wb) + (dot(hb, w_lo) + dot(h_lo, wb)) + rb_ref[...]
    lane = lax.broadcasted_iota(jnp.int32, logits.shape, 1)
    logits = jnp.where(lane < N_EXP, logits, -jnp.inf)
    col = lax.broadcasted_iota(jnp.int32, idx_ref.shape, 1)
    idx_out = jnp.zeros(idx_ref.shape, jnp.int32)
    val_out = jnp.zeros(idx_ref.shape, F32)
    cur = logits
    for k in range(TOP_K):
        mx = jnp.max(cur, axis=-1, keepdims=True)
        ix = jnp.min(jnp.where(cur == mx, lane, N_EXP), axis=-1, keepdims=True)
        idx_out = jnp.where(col == k, ix, idx_out)
        val_out = jnp.where(col == k, mx, val_out)
        cur = jnp.where(lane == ix, -jnp.inf, cur)
    gate_ref[...] = _softmax(val_out)
    idx_ref[...] = idx_out


def norm_router(x, g, mod, router_w, router_b, l):
    tm = 512
    sd = jax.ShapeDtypeStruct
    return pl.pallas_call(
        _norm_router_kernel,
        grid=(T // tm,),
        in_specs=[pl.BlockSpec((tm, D), lambda m: (m, 0)),
                  pl.BlockSpec((None, 1, D), lambda m: (l, 0, 0)),
                  pl.BlockSpec((None, None, 6, D), lambda m: (l, _grp_of_row(m * tm), 0, 0)),
                  pl.BlockSpec((None, D, LANE), lambda m: (l, 0, 0)),
                  pl.BlockSpec((None, 1, LANE), lambda m: (l, 0, 0))],
        out_specs=[pl.BlockSpec((tm, D), lambda m: (m, 0)),
                   pl.BlockSpec((tm, TOP_K), lambda m: (m, 0)),
                   pl.BlockSpec((tm, TOP_K), lambda m: (m, 0))],
        out_shape=[sd((T, D), F32), sd((T, TOP_K), jnp.int32), sd((T, TOP_K), F32)],
        compiler_params=_cp("parallel"),
        name="norm_router",
    )(x, g.reshape(DEPTH, 1, D), mod,
      jnp.pad(router_w, ((0, 0), (0, 0), (0, LANE - N_EXP))),
      jnp.pad(router_b.reshape(DEPTH, 1, N_EXP), ((0, 0), (0, 0), (0, LANE - N_EXP))))


def _moe_kernel(re_ref, b0_ref, nb_ref, cnt_ref, nr_ref, x_ref, wg_ref, wu_ref, bg_ref, bu_ref,
                wo_ref, bo_ref, o_ref, act_ref, wgb_ref, wub_ref, wob_ref):
    del re_ref
    r = pl.program_id(0)
    j = pl.program_id(1)
    i = pl.program_id(2)
    live = r < nr_ref[0]
    dot = lambda a, b: jnp.dot(a, b, preferred_element_type=F32)

    @pl.when(live & (i == 0) & (j < MOE_NF))
    def _():
        wgb_ref[...] = wg_ref[...].astype(BF16)
        wub_ref[...] = wu_ref[...].astype(BF16)

    @pl.when(live & (i == 0) & (j >= MOE_NF))
    def _():
        wob_ref[...] = wo_ref[...].astype(BF16)

    @pl.when(live & (i < nb_ref[r]))
    def _():
        nsub = (cnt_ref[b0_ref[r] + i] + (MOE_SUB - 1)) // MOE_SUB
        base = i * MOE_BM

        @pl.when(j < MOE_NF)
        def _():
            def sub(s, _):
                r0 = pl.multiple_of(s * MOE_SUB, MOE_SUB)
                rb = pl.multiple_of(base + s * MOE_SUB, MOE_SUB)
                x = x_ref[pl.ds(r0, MOE_SUB), :].astype(BF16)
                g = jnp.minimum(dot(x, wgb_ref[...]) + bg_ref[...], LIMIT)
                up = jnp.clip(dot(x, wub_ref[...]) + bu_ref[...], -LIMIT, LIMIT)
                act = (up + 1.0) * (g * jax.nn.sigmoid(ALPHA * g))
                act_ref[jnp.minimum(j, MOE_NF - 1), pl.ds(rb, MOE_SUB), :] = act.astype(BF16)
                return 0

            lax.fori_loop(0, nsub, sub, 0)

        @pl.when(j >= MOE_NF)
        def _():
            def sub(s, _):
                r0 = pl.multiple_of(s * MOE_SUB, MOE_SUB)
                rb = pl.multiple_of(base + s * MOE_SUB, MOE_SUB)
                y = bo_ref[...] + dot(act_ref[0, pl.ds(rb, MOE_SUB), :], wob_ref[0:MOE_TF, :])
                for f in range(1, MOE_NF):
                    y = y + dot(act_ref[f, pl.ds(rb, MOE_SUB), :],
                                wob_ref[f * MOE_TF:(f + 1) * MOE_TF, :])
                o_ref[pl.ds(r0, MOE_SUB), :] = y
                return 0

            lax.fori_loop(0, nsub, sub, 0)

            def zero(s, _):
                r0 = pl.multiple_of(s * MOE_SUB, MOE_SUB)
                o_ref[pl.ds(r0, MOE_SUB), :] = jnp.zeros((MOE_SUB, MOE_TN), F32)
                return 0

            lax.fori_loop(nsub, MOE_BM // MOE_SUB, zero, 0)


def moe_experts(xs, run_e, run_b0, run_nb, block_cnt, n_runs, w_in, b_in, w_out, b_out, l):
    def f1(r, j, nr):
        return jnp.where(r < nr[0], jnp.minimum(j, MOE_NF - 1), MOE_NF - 1)

    def n1(r, j, nr):
        return jnp.where(r < nr[0], jnp.clip(j - MOE_NF, 0, MOE_NN - 1), MOE_NN - 1)

    def ii(r, i, nb):
        return jnp.minimum(i, nb[r] - 1)

    assert MOE_RUN == 1

    def x_map(r, j, i, re, b0, nb, cnt, nr):
        return (b0[r], 0)

    def o_map(r, j, i, re, b0, nb, cnt, nr):
        return (b0[r] + jnp.where(j >= MOE_NF, ii(r, i, nb), 0), n1(r, j, nr))

    grid_spec = pltpu.PrefetchScalarGridSpec(
        num_scalar_prefetch=5,
        grid=(MOE_RUNS, MOE_NF + MOE_NN, MOE_RUN),
        in_specs=[
            pl.BlockSpec((MOE_BM, D), x_map),
            pl.BlockSpec((None, None, D, MOE_TF),
                         lambda r, j, i, re, b0, nb, cnt, nr: (l, re[r], 0, f1(r, j, nr))),
            pl.BlockSpec((None, None, D, MOE_TF),
                         lambda r, j, i, re, b0, nb, cnt, nr: (l, re[r], 0, MOE_NF + f1(r, j, nr))),
            pl.BlockSpec((None, None, 1, MOE_TF),
                         lambda r, j, i, re, b0, nb, cnt, nr: (l, re[r], 0, f1(r, j, nr))),
            pl.BlockSpec((None, None, 1, MOE_TF),
                         lambda r, j, i, re, b0, nb, cnt, nr: (l, re[r], 0, MOE_NF + f1(r, j, nr))),
            pl.BlockSpec((None, None, FF, MOE_TN),
                         lambda r, j, i, re, b0, nb, cnt, nr: (l, re[r], 0, n1(r, j, nr))),
            pl.BlockSpec((None, None, 1, MOE_TN),
                         lambda r, j, i, re, b0, nb, cnt, nr: (l, re[r], 0, n1(r, j, nr))),
        ],
        out_specs=pl.BlockSpec((MOE_BM, MOE_TN), o_map),
        scratch_shapes=[pltpu.VMEM((MOE_NF, MOE_RUN * MOE_BM, MOE_TF), BF16),
                        pltpu.VMEM((D, MOE_TF), BF16), pltpu.VMEM((D, MOE_TF), BF16),
                        pltpu.VMEM((FF, MOE_TN), BF16)],
    )
    return pl.pallas_call(
        _moe_kernel,
        grid_spec=grid_spec,
        out_shape=jax.ShapeDtypeStruct((MOE_ROWS, D), F32),
        compiler_params=_cp("arbitrary", "arbitrary", "arbitrary", vmem=60 << 20),
        name="moe_experts",
    )(run_e, run_b0, run_nb, block_cnt, n_runs, xs, w_in, w_in,
      b_in.reshape(DEPTH, N_EXP, 1, 2 * FF), b_in.reshape(DEPTH, N_EXP, 1, 2 * FF),
      w_out, b_out.reshape(DEPTH, N_EXP, 1, D))


def _combine_kernel(x_ref, y0_ref, y1_ref, y2_ref, y3_ref, gate_ref, mod_ref, o_ref):
    gate = gate_ref[...]
    acc = gate[:, 0:1] * y0_ref[...]
    for k, y_ref in enumerate((y1_ref, y2_ref, y3_ref), start=1):
        acc = acc + gate[:, k:k + 1] * y_ref[...]
    o_ref[...] = x_ref[...] + mod_ref[5:6, :] * acc


def moe_combine(x, yg, gate, mod, l):
    tm = 256
    nm = T // tm
    yspec = lambda k: pl.BlockSpec((tm, D), lambda m: (k * nm + m, 0))
    return pl.pallas_call(
        _combine_kernel,
        grid=(nm,),
        in_specs=[pl.BlockSpec((tm, D), lambda m: (m, 0)),
                  yspec(0), yspec(1), yspec(2), yspec(3),
                  pl.BlockSpec((tm, TOP_K), lambda m: (m, 0)),
                  pl.BlockSpec((None, None, 6, D), lambda m: (l, _grp_of_row(m * tm), 0, 0))],
        out_specs=pl.BlockSpec((tm, D), lambda m: (m, 0)),
        out_shape=jax.ShapeDtypeStruct((T, D), F32),
        compiler_params=_cp("parallel"),
        name="moe_combine",
    )(x, yg, yg, yg, yg, gate, mod)


def moe_routing(top_idx):
    n_assign = T * TOP_K
    flat_e = top_idx.reshape(-1)
    onehot = (flat_e[:, None] == jnp.arange(N_EXP, dtype=jnp.int32)[None, :]).astype(jnp.int32)
    csum = jnp.cumsum(onehot, axis=0)
    rank = jnp.take_along_axis(csum, flat_e[:, None], axis=1)[:, 0] - 1
    counts = csum[-1]
    nblk = (counts + MOE_BM - 1) // MOE_BM
    blk_end = jnp.cumsum(nblk)
    blk_start = blk_end - nblk
    n_used = blk_end[-1]
    dest = blk_start[flat_e] * MOE_BM + rank
    row_tok = jnp.zeros((MOE_ROWS,), jnp.int32).at[dest].set(
        jnp.arange(n_assign, dtype=jnp.int32) // TOP_K)
    mids = jnp.minimum(jnp.arange(MOE_BLOCKS, dtype=jnp.int32), n_used - 1)
    block_e = jnp.minimum(jnp.searchsorted(blk_end, mids, side='right'), N_EXP - 1).astype(jnp.int32)
    block_cnt = jnp.clip(counts[block_e] - (mids - blk_start[block_e]) * MOE_BM, 0, MOE_BM)
    block_cnt = jnp.where(jnp.arange(MOE_BLOCKS) < n_used, block_cnt, 0).astype(jnp.int32)
    nrun = (nblk + MOE_RUN - 1) // MOE_RUN
    run_end = jnp.cumsum(nrun)
    run_start = run_end - nrun
    n_runs = run_end[-1]
    ridx = jnp.arange(MOE_RUNS, dtype=jnp.int32)
    rids = jnp.minimum(ridx, n_runs - 1)
    run_e = jnp.minimum(jnp.searchsorted(run_end, rids, side='right'), N_EXP - 1).astype(jnp.int32)
    first = (rids - run_start[run_e]) * MOE_RUN
    run_b0 = blk_start[run_e] + first
    run_nb = jnp.minimum(MOE_RUN, nblk[run_e] - first)
    used = ridx < n_runs
    run_b0 = jnp.where(used, run_b0, run_b0 + run_nb - 1).astype(jnp.int32)
    run_nb = jnp.where(used, run_nb, 1).astype(jnp.int32)
    return row_tok, dest, run_e, run_b0, run_nb, block_cnt, n_runs.reshape(1).astype(jnp.int32)


def trunk_layer(x, l, mod, p, ssm_p, rope, cache_k, cache_v, h0):
    lambda_init = 0.8 - 0.6 * math.exp(-0.3 * l)
    zero_b = lambda n: jnp.zeros((DEPTH, 1, n), F32)

    h = norm_mod(x, p['norm_mix'], mod, l)
    z = matmul(h, p['w_in'], zero_b(IN_W), l, name="w_in")

    att = jnp.concatenate(
        [attn_ctx(z, p['diff_lambda'], p['diff_subln'], l, lambda_init),
         attn_lat(z, cache_k, cache_v, rope[0], rope[1], p['diff_lambda'], p['diff_subln'],
                  l, lambda_init)], axis=0)
    att_out = matmul(att, p['w_attn_out'], zero_b(D), l, out_dtype=BF16, name="attn_out")

    ab, bd, cd = ssm_p
    u_tm = jnp.transpose(z.reshape(N_BLK, BLK, IN_W)[:, :, 3 * QKV_W:3 * QKV_W + SSM_W], (1, 0, 2))
    y_tm, fin = ssm_mixer(u_tm, bd, cd, ab, p['ssm_d'].reshape(DEPTH, N_CC, 1, U_CH), h0, l)
    y = jnp.transpose(y_tm, (1, 0, 2)).reshape(T, SSM_W).astype(BF16)
    ssm_out = matmul_glu(y, p['w_ssm_glu'], l)

    cnv = conv_module(z, p['conv_w'], p['conv_b'], p['conv_ln_g'], p['conv_ln_b'], l)
    conv_out = matmul(cnv, p['w_conv_out'], p['b_conv_out'].reshape(DEPTH, 1, D), l,
                      out_dtype=BF16, name="conv_out")

    mixed = merge_gates(h, p['w_merge_gate'], p['b_merge_gate'].reshape(DEPTH, 1, 3 * D),
                        att_out, ssm_out, conv_out, l)
    x = matmul_resid(mixed, p['w_out'], x, mod, l, gate_row=2)

    h2p, top_idx, gate = norm_router(x, p['norm_ffn'], mod, p['router_w'], p['router_b'], l)
    row_tok, dest, run_e, run_b0, run_nb, block_cnt, n_runs = moe_routing(top_idx)
    xs = h2p.at[row_tok].get(mode='promise_in_bounds')
    ys = moe_experts(xs, run_e, run_b0, run_nb, block_cnt, n_runs,
                     p['moe_w_in'], p['moe_b_in'], p['moe_w_out'], p['moe_b_out'], l)
    dest_k = dest.reshape(T, TOP_K).T.reshape(-1)
    yg = ys.at[dest_k].get(mode='promise_in_bounds')
    x = moe_combine(x, yg, gate, mod, l)

    k_new = z[:T_CTX, QKV_W:2 * QKV_W].reshape(BATCH, SEQ, 2 * HEADS, HEAD_DIM)
    v_new = z[:T_CTX, 2 * QKV_W:3 * QKV_W].reshape(BATCH, SEQ, HEADS, 2 * HEAD_DIM)
    s_new = jnp.transpose(fin[:, :, :BATCH, :], (2, 0, 1, 3)).reshape(BATCH, 2, 2, SSM_G, SSM_N)
    return x, k_new, v_new, s_new


def kernel(x_prompt, x_sample, cache_k, cache_v, state_ssm, c, c_ctx, w_ada, b_ada, norm_mix, norm_ffn, w_in, diff_lambda, diff_subln, w_attn_out, ssm_a_re, ssm_a_im, ssm_log_dt, ssm_b, ssm_c, ssm_d, w_ssm_glu, conv_w, conv_b, conv_ln_g, conv_ln_b, w_conv_out, b_conv_out, w_merge_gate, b_merge_gate, w_out, router_w, router_b, moe_w_in, moe_b_in, moe_w_out, moe_b_out, norm_final):
    p = dict(norm_mix=norm_mix, norm_ffn=norm_ffn, w_in=w_in, diff_lambda=diff_lambda,
             diff_subln=diff_subln, w_attn_out=w_attn_out, ssm_d=ssm_d, w_ssm_glu=w_ssm_glu,
             conv_w=conv_w, conv_b=conv_b, conv_ln_g=conv_ln_g, conv_ln_b=conv_ln_b,
             w_conv_out=w_conv_out, b_conv_out=b_conv_out, w_merge_gate=w_merge_gate,
             b_merge_gate=b_merge_gate, w_out=w_out, router_w=router_w, router_b=router_b,
             moe_w_in=moe_w_in, moe_b_in=moe_b_in, moe_w_out=moe_w_out, moe_b_out=moe_b_out)

    cond8 = jnp.zeros((SUB, D), F32).at[0].set(c_ctx).at[1:1 + DEC_BATCH].set(c)
    mod = ada_all(cond8, w_ada, b_ada).reshape(DEPTH, SUB, 6, D)

    ab_re, ab_im, bb_re, bb_im = ssm_discretize(ssm_a_re, ssm_a_im, ssm_log_dt, ssm_b)
    ssm_p = ssm_weights(ab_re, ab_im, bb_re, bb_im, ssm_c)
    rope = rope_tables()

    ck = cache_k.reshape(DEC_BATCH, DEPTH, PAST, QKV_W)
    cv = cache_v.reshape(DEC_BATCH, DEPTH, PAST, QKV_W)
    st = state_ssm.reshape(DEC_BATCH, DEPTH, 2, 2, 1, SSM_G * SSM_N)
    h0_all = jnp.concatenate([jnp.zeros((N_SG - DEC_BATCH,) + st.shape[1:], F32), st], axis=0)

    x = jnp.concatenate([x_prompt.reshape(T_CTX, D), x_sample.reshape(T_LAT, D)], axis=0)
    new_k, new_v, new_s = [], [], []
    for l in range(DEPTH):
        x, k_l, v_l, s_l = trunk_layer(x, l, mod, p, ssm_p, rope, ck, cv, h0_all[:, l])
        new_k.append(k_l)
        new_v.append(v_l)
        new_s.append(s_l)
    y = final_norm(x, norm_final)
    return (y[:T_CTX].reshape(BATCH, SEQ, D), y[T_CTX:].reshape(DEC_BATCH, DEC_SEQ, D),
            jnp.stack(new_k, axis=1), jnp.stack(new_v, axis=1), jnp.stack(new_s, axis=1))
```

```python
import functools
import math

import jax
import jax.numpy as jnp
from jax import lax
from jax.experimental import pallas as pl
from jax.experimental.pallas import tpu as pltpu

F32 = jnp.float32
BF16 = jnp.bfloat16

D = 2048
BATCH = 32
SEQ = 256
DEPTH = 4
DEC_BATCH = 2
DEC_SEQ = 2048
PAST = 512
GRID_W = 64
HEADS = 8
HEAD_DIM = 64
ROPE_PAIRS = 16
ROPE_BASE = 10000.0
QKV_W = 1024
SSM_W = 1024
SSM_G = 64
SSM_P = 16
SSM_N = 64
CONV_CH = 1024
CONV_K = 31
IN_W = 6144
N_EXP = 32
TOP_K = 4
FF = 2048
LIMIT = 7.0
ALPHA = 1.702
EPS = 1e-6

T_CTX = BATCH * SEQ
T_LAT = DEC_BATCH * DEC_SEQ
T = T_CTX + T_LAT
BLK = 256
N_BLK = T // BLK
CTX_BLKS = T_CTX // BLK
LAT_BLKS = DEC_SEQ // BLK
N_GRP = 1 + DEC_BATCH
SUB = 8
LANE = 128
STATE_CH = 1024
U_CH = 256
N_CC = SSM_W // U_CH
N_SG = N_BLK // SUB

MOE_BM = 1024
MOE_SUB = 256
MOE_TF = 512
MOE_NF = FF // MOE_TF
MOE_TN = 512
MOE_NN = D // MOE_TN
MOE_RUN = 1
MOE_RUNS = (T * TOP_K) // (MOE_RUN * MOE_BM) + N_EXP
MOE_BLOCKS = (T * TOP_K) // MOE_BM + N_EXP
MOE_ROWS = MOE_BLOCKS * MOE_BM

VMEM_LIMIT = 56 << 20


def _cp(*sem, vmem=VMEM_LIMIT):
    return pltpu.CompilerParams(dimension_semantics=sem, vmem_limit_bytes=vmem)


def _grp_of_row(row):
    return jnp.where(row < T_CTX, 0, 1 + (row - T_CTX) // DEC_SEQ)


def _ada_kernel(c_ref, w_ref, b_ref, o_ref):
    c = c_ref[...]
    a = (c * jax.nn.sigmoid(c)).astype(BF16)
    o_ref[...] = jnp.dot(a, w_ref[...].astype(BF16), preferred_element_type=F32) + b_ref[...]


def ada_all(cond8, w_ada, b_ada):
    tn = 1024
    n = 6 * D
    return pl.pallas_call(
        _ada_kernel,
        grid=(DEPTH, n // tn),
        in_specs=[pl.BlockSpec((SUB, D), lambda l, j: (0, 0)),
                  pl.BlockSpec((None, D, tn), lambda l, j: (l, 0, j)),
                  pl.BlockSpec((None, 1, tn), lambda l, j: (l, 0, j))],
        out_specs=pl.BlockSpec((None, SUB, tn), lambda l, j: (l, 0, j)),
        out_shape=jax.ShapeDtypeStruct((DEPTH, SUB, n), F32),
        compiler_params=_cp("parallel", "parallel"),
        name="ada",
    )(cond8, w_ada, b_ada.reshape(DEPTH, 1, n))


def _rms(x, g):
    return x * lax.rsqrt(jnp.mean(x * x, axis=-1, keepdims=True) + EPS) * g


def _norm_mod_kernel(x_ref, g_ref, mod_ref, o_ref):
    y = _rms(x_ref[...], g_ref[...])
    sh = mod_ref[0:1, :]
    sc = mod_ref[1:2, :]
    o_ref[...] = (y * (1.0 + sc) + sh).astype(o_ref.dtype)


def norm_mod(x, g, mod, l):
    tm = 512
    return pl.pallas_call(
        _norm_mod_kernel,
        grid=(T // tm,),
        in_specs=[pl.BlockSpec((tm, D), lambda m: (m, 0)),
                  pl.BlockSpec((None, 1, D), lambda m: (l, 0, 0)),
                  pl.BlockSpec((None, None, 6, D), lambda m: (l, _grp_of_row(m * tm), 0, 0))],
        out_specs=pl.BlockSpec((tm, D), lambda m: (m, 0)),
        out_shape=jax.ShapeDtypeStruct((T, D), BF16),
        compiler_params=_cp("parallel"),
        name="norm_mod",
    )(x, g.reshape(DEPTH, 1, D), mod)


def _final_norm_kernel(x_ref, g_ref, o_ref):
    o_ref[...] = _rms(x_ref[...], g_ref[...])


def final_norm(x, g):
    tm = 512
    return pl.pallas_call(
        _final_norm_kernel,
        grid=(T // tm,),
        in_specs=[pl.BlockSpec((tm, D), lambda m: (m, 0)),
                  pl.BlockSpec((1, D), lambda m: (0, 0))],
        out_specs=pl.BlockSpec((tm, D), lambda m: (m, 0)),
        out_shape=jax.ShapeDtypeStruct((T, D), F32),
        compiler_params=_cp("parallel"),
        name="final_norm",
    )(x, g.reshape(1, D))


def _mm_kernel(a_ref, w_ref, b_ref, o_ref, wb_ref):
    @pl.when(pl.program_id(1) == 0)
    def _():
        wb_ref[...] = w_ref[...].astype(BF16)

    acc = jnp.dot(a_ref[...], wb_ref[...], preferred_element_type=F32)
    o_ref[...] = (acc + b_ref[...]).astype(o_ref.dtype)


def matmul(a, w, bias, l, *, tm=1024, tn=1024, out_dtype=F32, name="mm"):
    k, n = w.shape[1], w.shape[2]
    return pl.pallas_call(
        _mm_kernel,
        grid=(n // tn, T // tm),
        in_specs=[pl.BlockSpec((tm, k), lambda j, m: (m, 0)),
                  pl.BlockSpec((None, k, tn), lambda j, m: (l, 0, j)),
                  pl.BlockSpec((None, 1, tn), lambda j, m: (l, 0, j))],
        out_specs=pl.BlockSpec((tm, tn), lambda j, m: (m, j)),
        out_shape=jax.ShapeDtypeStruct((T, n), out_dtype),
        scratch_shapes=[pltpu.VMEM((k, tn), BF16)],
        compiler_params=_cp("parallel", "arbitrary"),
        name=name,
    )(a, w, bias)


def _mm_glu_kernel(a_ref, wa_ref, wg_ref, o_ref, wab_ref, wgb_ref):
    @pl.when(pl.program_id(1) == 0)
    def _():
        wab_ref[...] = wa_ref[...].astype(BF16)
        wgb_ref[...] = wg_ref[...].astype(BF16)

    a = a_ref[...]
    va = jnp.dot(a, wab_ref[...], preferred_element_type=F32)
    vg = jnp.dot(a, wgb_ref[...], preferred_element_type=F32)
    o_ref[...] = (va * jax.nn.sigmoid(vg)).astype(o_ref.dtype)


def matmul_glu(a, w, l, *, tm=1024, tn=512):
    k, n = w.shape[1], w.shape[2] // 2
    nj = n // tn
    return pl.pallas_call(
        _mm_glu_kernel,
        grid=(nj, T // tm),
        in_specs=[pl.BlockSpec((tm, k), lambda j, m: (m, 0)),
                  pl.BlockSpec((None, k, tn), lambda j, m: (l, 0, j)),
                  pl.BlockSpec((None, k, tn), lambda j, m: (l, 0, nj + j))],
        out_specs=pl.BlockSpec((tm, tn), lambda j, m: (m, j)),
        out_shape=jax.ShapeDtypeStruct((T, n), BF16),
        scratch_shapes=[pltpu.VMEM((k, tn), BF16), pltpu.VMEM((k, tn), BF16)],
        compiler_params=_cp("parallel", "arbitrary"),
        name="ssm_glu",
    )(a, w, w)


def _mm_resid_kernel(a_ref, w_ref, x_ref, mod_ref, o_ref, wb_ref, *, gate_row):
    @pl.when(pl.program_id(1) == 0)
    def _():
        wb_ref[...] = w_ref[...].astype(BF16)

    acc = jnp.dot(a_ref[...], wb_ref[...], preferred_element_type=F32)
    o_ref[...] = x_ref[...] + mod_ref[gate_row:gate_row + 1, :] * acc


def matmul_resid(a, w, x, mod, l, *, gate_row, tm=1024, tn=1024):
    k, n = w.shape[1], w.shape[2]
    return pl.pallas_call(
        functools.partial(_mm_resid_kernel, gate_row=gate_row),
        grid=(n // tn, T // tm),
        in_specs=[pl.BlockSpec((tm, k), lambda j, m: (m, 0)),
                  pl.BlockSpec((None, k, tn), lambda j, m: (l, 0, j)),
                  pl.BlockSpec((tm, tn), lambda j, m: (m, j)),
                  pl.BlockSpec((None, None, 6, tn), lambda j, m: (l, _grp_of_row(m * tm), 0, j))],
        out_specs=pl.BlockSpec((tm, tn), lambda j, m: (m, j)),
        out_shape=jax.ShapeDtypeStruct((T, n), F32),
        scratch_shapes=[pltpu.VMEM((k, tn), BF16)],
        compiler_params=_cp("parallel", "arbitrary"),
        name="out_resid",
    )(a, w, x, mod)


def _merge_kernel(h_ref, w0_ref, w1_ref, w2_ref, b0_ref, b1_ref, b2_ref,
                  att_ref, ssm_ref, cnv_ref, o_ref, wb0_ref, wb1_ref, wb2_ref):
    @pl.when(pl.program_id(1) == 0)
    def _():
        wb0_ref[...] = w0_ref[...].astype(BF16)
        wb1_ref[...] = w1_ref[...].astype(BF16)
        wb2_ref[...] = w2_ref[...].astype(BF16)

    h = h_ref[...]

    def gate(wb_ref, b_ref):
        return jax.nn.sigmoid(jnp.dot(h, wb_ref[...], preferred_element_type=F32) + b_ref[...])

    mixed = (gate(wb0_ref, b0_ref) * att_ref[...] + gate(wb1_ref, b1_ref) * ssm_ref[...]
             + gate(wb2_ref, b2_ref) * cnv_ref[...])
    o_ref[...] = mixed.astype(o_ref.dtype)


def merge_gates(h, w, b, att, ssm, cnv, l, *, tm=512, tn=512):
    nj = D // tn
    wspec = lambda br: pl.BlockSpec((None, D, tn), lambda j, m: (l, 0, br * nj + j))
    bspec = lambda br: pl.BlockSpec((None, 1, tn), lambda j, m: (l, 0, br * nj + j))
    tile = pl.BlockSpec((tm, tn), lambda j, m: (m, j))
    return pl.pallas_call(
        _merge_kernel,
        grid=(nj, T // tm),
        in_specs=[pl.BlockSpec((tm, D), lambda j, m: (m, 0)),
                  wspec(0), wspec(1), wspec(2), bspec(0), bspec(1), bspec(2),
                  tile, tile, tile],
        out_specs=tile,
        out_shape=jax.ShapeDtypeStruct((T, D), BF16),
        scratch_shapes=[pltpu.VMEM((D, tn), BF16)] * 3,
        compiler_params=_cp("parallel", "arbitrary"),
        name="merge_gates",
    )(h, w, w, w, b, b, b, att, ssm, cnv)


def _lambda(lam_ref, lambda_init):
    lp = lam_ref[...]
    s01 = jnp.sum(lp[0:1, :] * lp[1:2, :], axis=-1, keepdims=True)
    s23 = jnp.sum(lp[2:3, :] * lp[3:4, :], axis=-1, keepdims=True)
    return jnp.exp(s01) - jnp.exp(s23) + lambda_init


def _softmax(s):
    e = jnp.exp(s - jnp.max(s, axis=-1, keepdims=True))
    return e * (1.0 / jnp.sum(e, axis=-1, keepdims=True))


def _diff_attn_head(q, kb, vb, lam, g, lambda_init):
    lane = lax.broadcasted_iota(jnp.int32, q.shape, 1)
    q_a = jnp.where(lane < HEAD_DIM, q, 0.0).astype(BF16)
    q_b = jnp.where(lane >= HEAD_DIM, q, 0.0).astype(BF16)
    nt = (((1,), (1,)), ((), ()))
    def exp_rows(s):
        e = jnp.exp(s - jnp.max(s, axis=-1, keepdims=True))
        return e.astype(BF16), 1.0 / jnp.sum(e, axis=-1, keepdims=True)

    e_a, r_a = exp_rows(lax.dot_general(q_a, kb, nt, preferred_element_type=F32))
    e_b, r_b = exp_rows(lax.dot_general(q_b, kb, nt, preferred_element_type=F32))
    o = (jnp.dot(e_a, vb, preferred_element_type=F32) * r_a
         - jnp.dot(e_b, vb, preferred_element_type=F32) * (lam * r_b))
    return _rms(o, g) * (1.0 - lambda_init)


def _attn_ctx_kernel(q_ref, k_ref, v_ref, lam_ref, g_ref, o_ref, *, lambda_init):
    lam = _lambda(lam_ref, lambda_init)
    g = g_ref[...]
    for h in range(HEADS):
        cols = slice(h * LANE, (h + 1) * LANE)
        q = q_ref[:, cols] * (HEAD_DIM ** -0.5)
        o = _diff_attn_head(q, k_ref[:, cols].astype(BF16), v_ref[:, cols].astype(BF16),
                            lam, g, lambda_init)
        o_ref[:, cols] = o.astype(o_ref.dtype)


def attn_ctx(z, lam_p, subln, l, lambda_init):
    return pl.pallas_call(
        functools.partial(_attn_ctx_kernel, lambda_init=lambda_init),
        grid=(BATCH,),
        in_specs=[pl.BlockSpec((SEQ, QKV_W), lambda b: (b, 0)),
                  pl.BlockSpec((SEQ, QKV_W), lambda b: (b, 1)),
                  pl.BlockSpec((SEQ, QKV_W), lambda b: (b, 2)),
                  pl.BlockSpec((None, 4, HEAD_DIM), lambda b: (l, 0, 0)),
                  pl.BlockSpec((None, 1, LANE), lambda b: (l, 0, 0))],
        out_specs=pl.BlockSpec((SEQ, QKV_W), lambda b: (b, 0)),
        out_shape=jax.ShapeDtypeStruct((T_CTX, QKV_W), BF16),
        compiler_params=_cp("parallel"),
        name="attn_ctx",
    )(z, z, z, lam_p, subln.reshape(DEPTH, 1, LANE))


def _rope(x, cos, sin_signed):
    lane = lax.broadcasted_iota(jnp.int32, x.shape, 1)
    first = (lane % (2 * ROPE_PAIRS)) < ROPE_PAIRS
    partner = jnp.where(first, pltpu.roll(x, LANE - ROPE_PAIRS, 1), pltpu.roll(x, ROPE_PAIRS, 1))
    return x * cos + partner * sin_signed


def _attn_lat_kernel(q_ref, k_ref, v_ref, ck_ref, cv_ref, cq_ref, sq_ref, ckk_ref, skk_ref,
                     lam_ref, g_ref, o_ref, kb_ref, vb_ref, *, lambda_init):
    @pl.when(pl.program_id(2) == 0)
    def _():
        kb_ref[0:DEC_SEQ, :] = _rope(k_ref[...], ckk_ref[...], skk_ref[...]).astype(BF16)
        kb_ref[DEC_SEQ:DEC_SEQ + PAST, :] = ck_ref[...].astype(BF16)
        vb_ref[0:DEC_SEQ, :] = v_ref[...].astype(BF16)
        vb_ref[DEC_SEQ:DEC_SEQ + PAST, :] = cv_ref[...].astype(BF16)

    q = _rope(q_ref[...], cq_ref[...], sq_ref[...]) * (HEAD_DIM ** -0.5)
    o = _diff_attn_head(q, kb_ref[...], vb_ref[...], _lambda(lam_ref, lambda_init), g_ref[...],
                        lambda_init)
    o_ref[...] = o.astype(o_ref.dtype)


def attn_lat(z, cache_k, cache_v, cos_t, sin_t, lam_p, subln, l, lambda_init):
    tq = BLK
    nq = DEC_SEQ // tq
    qrow = lambda b, h, i: CTX_BLKS + b * nq + i
    krow = lambda b: T_CTX // DEC_SEQ + b
    return pl.pallas_call(
        functools.partial(_attn_lat_kernel, lambda_init=lambda_init),
        grid=(DEC_BATCH, HEADS, nq),
        in_specs=[pl.BlockSpec((tq, LANE), lambda b, h, i: (qrow(b, h, i), h)),
                  pl.BlockSpec((DEC_SEQ, LANE), lambda b, h, i: (krow(b), HEADS + h)),
                  pl.BlockSpec((DEC_SEQ, LANE), lambda b, h, i: (krow(b), 2 * HEADS + h)),
                  pl.BlockSpec((None, None, PAST, LANE), lambda b, h, i: (b, l, 0, h)),
                  pl.BlockSpec((None, None, PAST, LANE), lambda b, h, i: (b, l, 0, h)),
                  pl.BlockSpec((tq, LANE), lambda b, h, i: (i, 0)),
                  pl.BlockSpec((tq, LANE), lambda b, h, i: (i, 0)),
                  pl.BlockSpec((DEC_SEQ, LANE), lambda b, h, i: (0, 0)),
                  pl.BlockSpec((DEC_SEQ, LANE), lambda b, h, i: (0, 0)),
                  pl.BlockSpec((None, 4, HEAD_DIM), lambda b, h, i: (l, 0, 0)),
                  pl.BlockSpec((None, 1, LANE), lambda b, h, i: (l, 0, 0))],
        out_specs=pl.BlockSpec((tq, LANE), lambda b, h, i: (b * nq + i, h)),
        out_shape=jax.ShapeDtypeStruct((T_LAT, QKV_W), BF16),
        scratch_shapes=[pltpu.VMEM((DEC_SEQ + PAST, LANE), BF16)] * 2,
        compiler_params=_cp("parallel", "parallel", "arbitrary"),
        name="attn_lat",
    )(z, z, z, cache_k, cache_v, cos_t, sin_t, cos_t, sin_t, lam_p,
      subln.reshape(DEPTH, 1, LANE))


def rope_tables():
    row = (jnp.arange(DEC_SEQ) // GRID_W).astype(F32)
    col = (jnp.arange(DEC_SEQ) % GRID_W).astype(F32)
    inv_freq = ROPE_BASE ** (-jnp.arange(ROPE_PAIRS, dtype=F32) / ROPE_PAIRS)
    ang = jnp.stack([row[:, None] * inv_freq, col[:, None] * inv_freq], axis=1)
    cos, sin = jnp.cos(ang), jnp.sin(ang)
    cos64 = jnp.concatenate([cos, cos], axis=-1).reshape(DEC_SEQ, HEAD_DIM)
    sin64 = jnp.concatenate([-sin, sin], axis=-1).reshape(DEC_SEQ, HEAD_DIM)
    return jnp.tile(cos64, (1, 2)), jnp.tile(sin64, (1, 2))


def _zoh_kernel(are_ref, aim_ref, ldt_ref, bre_ref, bim_ref,
                abr_ref, abi_ref, bbr_ref, bbi_ref):
    a_re = are_ref[...]
    a_im = aim_ref[...]
    dt = jnp.exp(ldt_ref[...])
    mag = jnp.exp(dt * a_re)
    ang = dt * a_im
    ab_re = mag * jnp.cos(ang)
    ab_im = mag * jnp.sin(ang)
    den = a_re * a_re + a_im * a_im
    xm = ab_re - 1.0
    f_re = (xm * a_re + ab_im * a_im) / den
    f_im = (ab_im * a_re - xm * a_im) / den
    abr_ref[...] = ab_re
    abi_ref[...] = ab_im
    rows = a_re.shape[0]
    fr = jnp.broadcast_to(f_re[:, None, :], (rows, SSM_P, SSM_N))
    fi = jnp.broadcast_to(f_im[:, None, :], (rows, SSM_P, SSM_N))
    b_re = bre_ref[...]
    b_im = bim_ref[...]
    bbr_ref[...] = fr * b_re - fi * b_im
    bbi_ref[...] = fr * b_im + fi * b_re


def ssm_discretize(a_re, a_im, log_dt, b):
    rows = DEPTH * 2 * SSM_G
    bt = jnp.swapaxes(b, -1, -2)
    b_re = bt[:, :, 0].reshape(rows, SSM_P, SSM_N)
    b_im = bt[:, :, 1].reshape(rows, SSM_P, SSM_N)
    sd = jax.ShapeDtypeStruct
    ab_re, ab_im, bb_re, bb_im = pl.pallas_call(
        _zoh_kernel,
        out_shape=[sd((rows, SSM_N), F32), sd((rows, SSM_N), F32),
                   sd((rows, SSM_P, SSM_N), F32), sd((rows, SSM_P, SSM_N), F32)],
        compiler_params=pltpu.CompilerParams(vmem_limit_bytes=VMEM_LIMIT),
        name="ssm_zoh",
    )(a_re.reshape(rows, SSM_N), a_im.reshape(rows, SSM_N), log_dt.reshape(rows, 1), b_re, b_im)
    return ab_re, ab_im, bb_re, bb_im


def _block_diag(m):
    eye = jnp.eye(16, dtype=m.dtype)
    out = m[..., :, :, None, :] * eye[:, None, :, None]
    return out.reshape(*m.shape[:-3], 16 * m.shape[-2], 16 * m.shape[-1])


def ssm_weights(ab_re, ab_im, bb_re, bb_im, c):
    ab = jnp.stack([ab_re, ab_im], axis=0).reshape(2, DEPTH, 2, N_CC, 1, STATE_CH)
    ab = jnp.transpose(ab, (1, 2, 0, 3, 4, 5))
    bb = jnp.stack([bb_re, bb_im], axis=0).reshape(2, DEPTH, 2, N_CC, 16, SSM_P, SSM_N)
    bd = _block_diag(jnp.transpose(bb, (1, 2, 0, 3, 4, 5, 6))).astype(BF16)
    ct = jnp.swapaxes(c, -1, -2).reshape(DEPTH, 2, 2, N_CC, 16, SSM_N, SSM_P)
    cd = _block_diag(ct).astype(BF16)
    return ab, bd, cd


def _cmul_add(ar, ai, xr, xi, br, bi):
    return ar * xr - ai * xi + br, ar * xi + ai * xr + bi


def _ssm_kernel(u_ref, bd_ref, cd_ref, ab_ref, d_ref, h0_ref, y_ref, fin_ref,
                xr_ref, xi_ref, tr_ref, ti_ref, acc_ref, cin_ref):
    nj = BLK
    is_lat = pl.program_id(0) >= CTX_BLKS // SUB
    u = u_ref[...].reshape(nj * SUB, U_CH)
    ub = u.astype(BF16)
    acc_ref[...] = u * d_ref[...]
    for d in range(2):
        xr_ref[...] = jnp.dot(ub, bd_ref[d, 0], preferred_element_type=F32)
        xi_ref[...] = jnp.dot(ub, bd_ref[d, 1], preferred_element_type=F32)
        a1r = ab_ref[d, 0]
        a1i = ab_ref[d, 1]
        ar = jnp.broadcast_to(a1r, (SUB, STATE_CH))
        ai = jnp.broadcast_to(a1i, (SUB, STATE_CH))

        def step(i, carry, d=d, ar=ar, ai=ai):
            j = i if d == 0 else nj - 1 - i
            r0 = pl.multiple_of(j * SUB, SUB)
            nr, ni = _cmul_add(ar, ai, carry[0], carry[1],
                               xr_ref[pl.ds(r0, SUB), :], xi_ref[pl.ds(r0, SUB), :])
            xr_ref[pl.ds(r0, SUB), :] = nr
            xi_ref[pl.ds(r0, SUB), :] = ni
            return nr, ni

        zero = jnp.zeros((SUB, STATE_CH), F32)
        lax.fori_loop(0, nj, step, (zero, zero), unroll=4)

        last = (nj - 1) * SUB if d == 0 else 0
        fin_ref[d, 0] = xr_ref[last:last + SUB, :]
        fin_ref[d, 1] = xi_ref[last:last + SUB, :]

        @pl.when(is_lat)
        def _(d=d, a1r=a1r, a1i=a1i, last=last):
            def pstep(j, p):
                tr_ref[pl.ds(j, 1), :] = p[0]
                ti_ref[pl.ds(j, 1), :] = p[1]
                return _cmul_add(a1r, a1i, p[0], p[1], 0.0, 0.0)

            lax.fori_loop(0, nj, pstep, (a1r, a1i))
            anr = tr_ref[nj - 1:nj, :]
            ani = ti_ref[nj - 1:nj, :]
            cr = [None] * SUB
            ci = [None] * SUB
            order = list(range(SUB)) if d == 0 else list(range(SUB - 1, -1, -1))
            cr[order[0]] = h0_ref[d, 0]
            ci[order[0]] = h0_ref[d, 1]
            for prev, cur in zip(order[:-1], order[1:]):
                fr = xr_ref[last + prev:last + prev + 1, :]
                fi = xi_ref[last + prev:last + prev + 1, :]
                cr[cur], ci[cur] = _cmul_add(anr, ani, cr[prev], ci[prev], fr, fi)
            for s in range(SUB):
                cin_ref[0, s:s + 1, :] = cr[s]
                cin_ref[1, s:s + 1, :] = ci[s]
            cin_r = cin_ref[0]
            cin_i = cin_ref[1]

            def fstep(i, _):
                t = i if d == 0 else nj - 1 - i
                r0 = pl.multiple_of(i * SUB, SUB)
                pr = tr_ref[pl.ds(t, 1), :]
                pi = ti_ref[pl.ds(t, 1), :]
                nr, ni = _cmul_add(pr, pi, cin_r, cin_i,
                                   xr_ref[pl.ds(r0, SUB), :], xi_ref[pl.ds(r0, SUB), :])
                xr_ref[pl.ds(r0, SUB), :] = nr
                xi_ref[pl.ds(r0, SUB), :] = ni
                return 0

            lax.fori_loop(0, nj, fstep, 0, unroll=2)

        acc_ref[...] += (jnp.dot(xr_ref[...].astype(BF16), cd_ref[d, 0], preferred_element_type=F32)
                         - jnp.dot(xi_ref[...].astype(BF16), cd_ref[d, 1], preferred_element_type=F32))

    y_ref[...] = jax.nn.gelu(acc_ref[...]).reshape(nj, SUB, U_CH)


def ssm_mixer(u_tm, bd, cd, ab, dvec, h0, l):
    nrow = BLK * SUB
    sd = jax.ShapeDtypeStruct
    return pl.pallas_call(
        _ssm_kernel,
        grid=(N_SG, N_CC),
        in_specs=[pl.BlockSpec((BLK, SUB, U_CH), lambda s, c: (0, s, c)),
                  pl.BlockSpec((None, 2, 2, None, U_CH, STATE_CH), lambda s, c: (l, 0, 0, c, 0, 0)),
                  pl.BlockSpec((None, 2, 2, None, STATE_CH, U_CH), lambda s, c: (l, 0, 0, c, 0, 0)),
                  pl.BlockSpec((None, 2, 2, None, 1, STATE_CH), lambda s, c: (l, 0, 0, c, 0, 0)),
                  pl.BlockSpec((None, None, 1, U_CH), lambda s, c: (l, c, 0, 0)),
                  pl.BlockSpec((None, 2, 2, 1, STATE_CH), lambda s, c: (s, 0, 0, 0, c))],
        out_specs=[pl.BlockSpec((BLK, SUB, U_CH), lambda s, c: (0, s, c)),
                   pl.BlockSpec((2, 2, SUB, STATE_CH), lambda s, c: (0, 0, s, c))],
        out_shape=[sd((BLK, N_BLK, SSM_W), F32), sd((2, 2, N_BLK, SSM_G * SSM_N), F32)],
        scratch_shapes=[pltpu.VMEM((nrow, STATE_CH), F32), pltpu.VMEM((nrow, STATE_CH), F32),
                        pltpu.VMEM((BLK, STATE_CH), F32), pltpu.VMEM((BLK, STATE_CH), F32),
                        pltpu.VMEM((nrow, U_CH), F32), pltpu.VMEM((2, SUB, STATE_CH), F32)],
        compiler_params=_cp("parallel", "parallel"),
        name="ssm",
    )(u_tm, bd, cd, ab, dvec, h0)


CONV_HALO = 16
CONV_RC = 32


def _conv_kernel(xa_ref, xg_ref, pa_ref, pg_ref, na_ref, ng_ref, w_ref, cb_ref, lg_ref, lb_ref,
                 o_ref, xp_ref, xs_ref):
    r = pl.program_id(0)
    is_lat = r >= CTX_BLKS
    pos = jnp.bitwise_and(r, LAT_BLKS - 1)
    pv = jnp.where(is_lat & (pos != 0), 1.0, 0.0)
    nv = jnp.where(is_lat & (pos != LAT_BLKS - 1), 1.0, 0.0)
    glu = lambda a_ref, g_ref: a_ref[...] * jax.nn.sigmoid(g_ref[...])
    xp_ref[0:CONV_HALO, :] = glu(pa_ref, pg_ref) * pv
    xp_ref[CONV_HALO:CONV_HALO + BLK, :] = glu(xa_ref, xg_ref)
    xp_ref[CONV_HALO + BLK:, :] = glu(na_ref, ng_ref) * nv
    span = BLK + 3 * SUB
    for s in range(SUB):
        xs_ref[s] = xp_ref[s:s + span, :]

    def chunk(c, _):
        r0 = pl.multiple_of(c * CONV_RC, CONV_RC)
        acc = jnp.zeros((CONV_RC, CONV_CH), F32)
        for k in range(CONV_K):
            off = k + CONV_HALO - CONV_K // 2
            acc = acc + xs_ref[off % SUB, pl.ds(r0 + (off // SUB) * SUB, CONV_RC), :] * w_ref[k:k + 1, :]
        acc = acc + cb_ref[...]
        mu = jnp.mean(acc, axis=-1, keepdims=True)
        xc = acc - mu
        var = jnp.mean(xc * xc, axis=-1, keepdims=True)
        yn = xc * lax.rsqrt(var + EPS) * lg_ref[...] + lb_ref[...]
        o_ref[pl.ds(r0, CONV_RC), :] = (yn * jax.nn.sigmoid(yn)).astype(o_ref.dtype)
        return 0

    lax.fori_loop(0, BLK // CONV_RC, chunk, 0)


def conv_module(z, conv_w, conv_b, ln_g, ln_b, l):
    hb = BLK // CONV_HALO
    last_halo = T // CONV_HALO - 1
    ca, cg = 4, 5
    prev = lambda r: jnp.maximum(r * hb - 1, 0)
    nxt = lambda r: jnp.minimum((r + 1) * hb, last_halo)
    vec = lambda: pl.BlockSpec((None, 1, CONV_CH), lambda r: (l, 0, 0))
    return pl.pallas_call(
        _conv_kernel,
        grid=(N_BLK,),
        in_specs=[pl.BlockSpec((BLK, CONV_CH), lambda r: (r, ca)),
                  pl.BlockSpec((BLK, CONV_CH), lambda r: (r, cg)),
                  pl.BlockSpec((CONV_HALO, CONV_CH), lambda r: (prev(r), ca)),
                  pl.BlockSpec((CONV_HALO, CONV_CH), lambda r: (prev(r), cg)),
                  pl.BlockSpec((CONV_HALO, CONV_CH), lambda r: (nxt(r), ca)),
                  pl.BlockSpec((CONV_HALO, CONV_CH), lambda r: (nxt(r), cg)),
                  pl.BlockSpec((None, CONV_K, CONV_CH), lambda r: (l, 0, 0)),
                  vec(), vec(), vec()],
        out_specs=pl.BlockSpec((BLK, CONV_CH), lambda r: (r, 0)),
        out_shape=jax.ShapeDtypeStruct((T, CONV_CH), BF16),
        scratch_shapes=[pltpu.VMEM((BLK + 2 * CONV_HALO, CONV_CH), F32),
                        pltpu.VMEM((SUB, BLK + 3 * SUB, CONV_CH), F32)],
        compiler_params=_cp("parallel"),
        name="conv",
    )(z, z, z, z, z, z, conv_w, conv_b.reshape(DEPTH, 1, CONV_CH),
      ln_g.reshape(DEPTH, 1, CONV_CH), ln_b.reshape(DEPTH, 1, CONV_CH))


def _norm_router_kernel(x_ref, g_ref, mod_ref, rw_ref, rb_ref, h_ref, idx_ref, gate_ref):
    y = _rms(x_ref[...], g_ref[...])
    h = y * (1.0 + mod_ref[4:5, :]) + mod_ref[3:4, :]
    h_ref[...] = h
    hb = h.astype(BF16)
    h_lo = (h - hb.astype(F32)).astype(BF16)
    w = rw_ref[...]
    wb = w.astype(BF16)
    w_lo = (w - wb.astype(F32)).astype(BF16)
    dot = lambda a, b: jnp.dot(a, b, preferred_element_type=F32)
    logits = dot(hb, wb) + (dot(hb, w_lo) + dot(h_lo, wb)) + rb_ref[...]
    lane = lax.broadcasted_iota(jnp.int32, logits.shape, 1)
    logits = jnp.where(lane < N_EXP, logits, -jnp.inf)
    col = lax.broadcasted_iota(jnp.int32, idx_ref.shape, 1)
    idx_out = jnp.zeros(idx_ref.shape, jnp.int32)
    val_out = jnp.zeros(idx_ref.shape, F32)
    cur = logits
    for k in range(TOP_K):
        mx = jnp.max(cur, axis=-1, keepdims=True)
        ix = jnp.min(jnp.where(cur == mx, lane, N_EXP), axis=-1, keepdims=True)
        idx_out = jnp.where(col == k, ix, idx_out)
        val_out = jnp.where(col == k, mx, val_out)
        cur = jnp.where(lane == ix, -jnp.inf, cur)
    gate_ref[...] = _softmax(val_out)
    idx_ref[...] = idx_out


def norm_router(x, g, mod, router_w, router_b, l):
    tm = 512
    sd = jax.ShapeDtypeStruct
    return pl.pallas_call(
        _norm_router_kernel,
        grid=(T // tm,),
        in_specs=[pl.BlockSpec((tm, D), lambda m: (m, 0)),
                  pl.BlockSpec((None, 1, D), lambda m: (l, 0, 0)),
                  pl.BlockSpec((None, None, 6, D), lambda m: (l, _grp_of_row(m * tm), 0, 0)),
                  pl.BlockSpec((None, D, LANE), lambda m: (l, 0, 0)),
                  pl.BlockSpec((None, 1, LANE), lambda m: (l, 0, 0))],
        out_specs=[pl.BlockSpec((tm, D), lambda m: (m, 0)),
                   pl.BlockSpec((tm, TOP_K), lambda m: (m, 0)),
                   pl.BlockSpec((tm, TOP_K), lambda m: (m, 0))],
        out_shape=[sd((T, D), F32), sd((T, TOP_K), jnp.int32), sd((T, TOP_K), F32)],
        compiler_params=_cp("parallel"),
        name="norm_router",
    )(x, g.reshape(DEPTH, 1, D), mod,
      jnp.pad(router_w, ((0, 0), (0, 0), (0, LANE - N_EXP))),
      jnp.pad(router_b.reshape(DEPTH, 1, N_EXP), ((0, 0), (0, 0), (0, LANE - N_EXP))))


def _moe_kernel(re_ref, b0_ref, nb_ref, cnt_ref, nr_ref, x_ref, wg_ref, wu_ref, bg_ref, bu_ref,
                wo_ref, bo_ref, o_ref, act_ref, wgb_ref, wub_ref, wob_ref):
    del re_ref
    r = pl.program_id(0)
    j = pl.program_id(1)
    i = pl.program_id(2)
    live = r < nr_ref[0]
    dot = lambda a, b: jnp.dot(a, b, preferred_element_type=F32)

    @pl.when(live & (i == 0) & (j < MOE_NF))
    def _():
        wgb_ref[...] = wg_ref[...].astype(BF16)
        wub_ref[...] = wu_ref[...].astype(BF16)

    @pl.when(live & (i == 0) & (j >= MOE_NF))
    def _():
        wob_ref[...] = wo_ref[...].astype(BF16)

    @pl.when(live & (i < nb_ref[r]))
    def _():
        nsub = (cnt_ref[b0_ref[r] + i] + (MOE_SUB - 1)) // MOE_SUB
        base = i * MOE_BM

        @pl.when(j < MOE_NF)
        def _():
            def sub(s, _):
                r0 = pl.multiple_of(s * MOE_SUB, MOE_SUB)
                rb = pl.multiple_of(base + s * MOE_SUB, MOE_SUB)
                x = x_ref[pl.ds(r0, MOE_SUB), :].astype(BF16)
                g = jnp.minimum(dot(x, wgb_ref[...]) + bg_ref[...], LIMIT)
                up = jnp.clip(dot(x, wub_ref[...]) + bu_ref[...], -LIMIT, LIMIT)
                act = (up + 1.0) * (g * jax.nn.sigmoid(ALPHA * g))
                act_ref[jnp.minimum(j, MOE_NF - 1), pl.ds(rb, MOE_SUB), :] = act.astype(BF16)
                return 0

            lax.fori_loop(0, nsub, sub, 0)

        @pl.when(j >= MOE_NF)
        def _():
            def sub(s, _):
                r0 = pl.multiple_of(s * MOE_SUB, MOE_SUB)
                rb = pl.multiple_of(base + s * MOE_SUB, MOE_SUB)
                y = bo_ref[...] + dot(act_ref[0, pl.ds(rb, MOE_SUB), :], wob_ref[0:MOE_TF, :])
                for f in range(1, MOE_NF):
                    y = y + dot(act_ref[f, pl.ds(rb, MOE_SUB), :],
                                wob_ref[f * MOE_TF:(f + 1) * MOE_TF, :])
                o_ref[pl.ds(r0, MOE_SUB), :] = y
                return 0

            lax.fori_loop(0, nsub, sub, 0)

            def zero(s, _):
                r0 = pl.multiple_of(s * MOE_SUB, MOE_SUB)
                o_ref[pl.ds(r0, MOE_SUB), :] = jnp.zeros((MOE_SUB, MOE_TN), F32)
                return 0

            lax.fori_loop(nsub, MOE_BM // MOE_SUB, zero, 0)


def moe_experts(xs, run_e, run_b0, run_nb, block_cnt, n_runs, w_in, b_in, w_out, b_out, l):
    def f1(r, j, nr):
        return jnp.where(r < nr[0], jnp.minimum(j, MOE_NF - 1), MOE_NF - 1)

    def n1(r, j, nr):
        return jnp.where(r < nr[0], jnp.clip(j - MOE_NF, 0, MOE_NN - 1), MOE_NN - 1)

    def ii(r, i, nb):
        return jnp.minimum(i, nb[r] - 1)

    assert MOE_RUN == 1

    def x_map(r, j, i, re, b0, nb, cnt, nr):
        return (b0[r], 0)

    def o_map(r, j, i, re, b0, nb, cnt, nr):
        return (b0[r] + jnp.where(j >= MOE_NF, ii(r, i, nb), 0), n1(r, j, nr))

    grid_spec = pltpu.PrefetchScalarGridSpec(
        num_scalar_prefetch=5,
        grid=(MOE_RUNS, MOE_NF + MOE_NN, MOE_RUN),
        in_specs=[
            pl.BlockSpec((MOE_BM, D), x_map),
            pl.BlockSpec((None, None, D, MOE_TF),
                         lambda r, j, i, re, b0, nb, cnt, nr: (l, re[r], 0, f1(r, j, nr))),
            pl.BlockSpec((None, None, D, MOE_TF),
                         lambda r, j, i, re, b0, nb, cnt, nr: (l, re[r], 0, MOE_NF + f1(r, j, nr))),
            pl.BlockSpec((None, None, 1, MOE_TF),
                         lambda r, j, i, re, b0, nb, cnt, nr: (l, re[r], 0, f1(r, j, nr))),
            pl.BlockSpec((None, None, 1, MOE_TF),
                         lambda r, j, i, re, b0, nb, cnt, nr: (l, re[r], 0, MOE_NF + f1(r, j, nr))),
            pl.BlockSpec((None, None, FF, MOE_TN),
                         lambda r, j, i, re, b0, nb, cnt, nr: (l, re[r], 0, n1(r, j, nr))),
            pl.BlockSpec((None, None, 1, MOE_TN),
                         lambda r, j, i, re, b0, nb, cnt, nr: (l, re[r], 0, n1(r, j, nr))),
        ],
        out_specs=pl.BlockSpec((MOE_BM, MOE_TN), o_map),
        scratch_shapes=[pltpu.VMEM((MOE_NF, MOE_RUN * MOE_BM, MOE_TF), BF16),
                        pltpu.VMEM((D, MOE_TF), BF16), pltpu.VMEM((D, MOE_TF), BF16),
                        pltpu.VMEM((FF, MOE_TN), BF16)],
    )
    return pl.pallas_call(
        _moe_kernel,
        grid_spec=grid_spec,
        out_shape=jax.ShapeDtypeStruct((MOE_ROWS, D), F32),
        compiler_params=_cp("arbitrary", "arbitrary", "arbitrary", vmem=60 << 20),
        name="moe_experts",
    )(run_e, run_b0, run_nb, block_cnt, n_runs, xs, w_in, w_in,
      b_in.reshape(DEPTH, N_EXP, 1, 2 * FF), b_in.reshape(DEPTH, N_EXP, 1, 2 * FF),
      w_out, b_out.reshape(DEPTH, N_EXP, 1, D))


def _combine_kernel(x_ref, y0_ref, y1_ref, y2_ref, y3_ref, gate_ref, mod_ref, o_ref):
    gate = gate_ref[...]
    acc = gate[:, 0:1] * y0_ref[...]
    for k, y_ref in enumerate((y1_ref, y2_ref, y3_ref), start=1):
        acc = acc + gate[:, k:k + 1] * y_ref[...]
    o_ref[...] = x_ref[...] + mod_ref[5:6, :] * acc


def moe_combine(x, yg, gate, mod, l):
    tm = 256
    nm = T // tm
    yspec = lambda k: pl.BlockSpec((tm, D), lambda m: (k * nm + m, 0))
    return pl.pallas_call(
        _combine_kernel,
        grid=(nm,),
        in_specs=[pl.BlockSpec((tm, D), lambda m: (m, 0)),
                  yspec(0), yspec(1), yspec(2), yspec(3),
                  pl.BlockSpec((tm, TOP_K), lambda m: (m, 0)),
                  pl.BlockSpec((None, None, 6, D), lambda m: (l, _grp_of_row(m * tm), 0, 0))],
        out_specs=pl.BlockSpec((tm, D), lambda m: (m, 0)),
        out_shape=jax.ShapeDtypeStruct((T, D), F32),
        compiler_params=_cp("parallel"),
        name="moe_combine",
    )(x, yg, yg, yg, yg, gate, mod)


def moe_routing(top_idx):
    n_assign = T * TOP_K
    flat_e = top_idx.reshape(-1)
    onehot = (flat_e[:, None] == jnp.arange(N_EXP, dtype=jnp.int32)[None, :]).astype(jnp.int32)
    csum = jnp.cumsum(onehot, axis=0)
    rank = jnp.take_along_axis(csum, flat_e[:, None], axis=1)[:, 0] - 1
    counts = csum[-1]
    nblk = (counts + MOE_BM - 1) // MOE_BM
    blk_end = jnp.cumsum(nblk)
    blk_start = blk_end - nblk
    n_used = blk_end[-1]
    dest = blk_start[flat_e] * MOE_BM + rank
    row_tok = jnp.zeros((MOE_ROWS,), jnp.int32).at[dest].set(
        jnp.arange(n_assign, dtype=jnp.int32) // TOP_K)
    mids = jnp.minimum(jnp.arange(MOE_BLOCKS, dtype=jnp.int32), n_used - 1)
    block_e = jnp.minimum(jnp.searchsorted(blk_end, mids, side='right'), N_EXP - 1).astype(jnp.int32)
    block_cnt = jnp.clip(counts[block_e] - (mids - blk_start[block_e]) * MOE_BM, 0, MOE_BM)
    block_cnt = jnp.where(jnp.arange(MOE_BLOCKS) < n_used, block_cnt, 0).astype(jnp.int32)
    nrun = (nblk + MOE_RUN - 1) // MOE_RUN
    run_end = jnp.cumsum(nrun)
    run_start = run_end - nrun
    n_runs = run_end[-1]
    ridx = jnp.arange(MOE_RUNS, dtype=jnp.int32)
    rids = jnp.minimum(ridx, n_runs - 1)
    run_e = jnp.minimum(jnp.searchsorted(run_end, rids, side='right'), N_EXP - 1).astype(jnp.int32)
    first = (rids - run_start[run_e]) * MOE_RUN
    run_b0 = blk_start[run_e] + first
    run_nb = jnp.minimum(MOE_RUN, nblk[run_e] - first)
    used = ridx < n_runs
    run_b0 = jnp.where(used, run_b0, run_b0 + run_nb - 1).astype(jnp.int32)
    run_nb = jnp.where(used, run_nb, 1).astype(jnp.int32)
    return row_tok, dest, run_e, run_b0, run_nb, block_cnt, n_runs.reshape(1).astype(jnp.int32)


def trunk_layer(x, l, mod, p, ssm_p, rope, cache_k, cache_v, h0):
    lambda_init = 0.8 - 0.6 * math.exp(-0.3 * l)
    zero_b = lambda n: jnp.zeros((DEPTH, 1, n), F32)

    h = norm_mod(x, p['norm_mix'], mod, l)
    z = matmul(h, p['w_in'], zero_b(IN_W), l, name="w_in")

    att = jnp.concatenate(
        [attn_ctx(z, p['diff_lambda'], p['diff_subln'], l, lambda_init),
         attn_lat(z, cache_k, cache_v, rope[0], rope[1], p['diff_lambda'], p['diff_subln'],
                  l, lambda_init)], axis=0)
    att_out = matmul(att, p['w_attn_out'], zero_b(D), l, out_dtype=BF16, name="attn_out")

    ab, bd, cd = ssm_p
    u_tm = jnp.transpose(z.reshape(N_BLK, BLK, IN_W)[:, :, 3 * QKV_W:3 * QKV_W + SSM_W], (1, 0, 2))
    y_tm, fin = ssm_mixer(u_tm, bd, cd, ab, p['ssm_d'].reshape(DEPTH, N_CC, 1, U_CH), h0, l)
    y = jnp.transpose(y_tm, (1, 0, 2)).reshape(T, SSM_W).astype(BF16)
    ssm_out = matmul_glu(y, p['w_ssm_glu'], l)

    cnv = conv_module(z, p['conv_w'], p['conv_b'], p['conv_ln_g'], p['conv_ln_b'], l)
    conv_out = matmul(cnv, p['w_conv_out'], p['b_conv_out'].reshape(DEPTH, 1, D), l,
                      out_dtype=BF16, name="conv_out")

    mixed = merge_gates(h, p['w_merge_gate'], p['b_merge_gate'].reshape(DEPTH, 1, 3 * D),
                        att_out, ssm_out, conv_out, l)
    x = matmul_resid(mixed, p['w_out'], x, mod, l, gate_row=2)

    h2p, top_idx, gate = norm_router(x, p['norm_ffn'], mod, p['router_w'], p['router_b'], l)
    row_tok, dest, run_e, run_b0, run_nb, block_cnt, n_runs = moe_routing(top_idx)
    xs = h2p.at[row_tok].get(mode='promise_in_bounds')
    ys = moe_experts(xs, run_e, run_b0, run_nb, block_cnt, n_runs,
                     p['moe_w_in'], p['moe_b_in'], p['moe_w_out'], p['moe_b_out'], l)
    dest_k = dest.reshape(T, TOP_K).T.reshape(-1)
    yg = ys.at[dest_k].get(mode='promise_in_bounds')
    x = moe_combine(x, yg, gate, mod, l)

    k_new = z[:T_CTX, QKV_W:2 * QKV_W].reshape(BATCH, SEQ, 2 * HEADS, HEAD_DIM)
    v_new = z[:T_CTX, 2 * QKV_W:3 * QKV_W].reshape(BATCH, SEQ, HEADS, 2 * HEAD_DIM)
    s_new = jnp.transpose(fin[:, :, :BATCH, :], (2, 0, 1, 3)).reshape(BATCH, 2, 2, SSM_G, SSM_N)
    return x, k_new, v_new, s_new


def kernel(x_prompt, x_sample, cache_k, cache_v, state_ssm, c, c_ctx, w_ada, b_ada, norm_mix, norm_ffn, w_in, diff_lambda, diff_subln, w_attn_out, ssm_a_re, ssm_a_im, ssm_log_dt, ssm_b, ssm_c, ssm_d, w_ssm_glu, conv_w, conv_b, conv_ln_g, conv_ln_b, w_conv_out, b_conv_out, w_merge_gate, b_merge_gate, w_out, router_w, router_b, moe_w_in, moe_b_in, moe_w_out, moe_b_out, norm_final):
    p = dict(norm_mix=norm_mix, norm_ffn=norm_ffn, w_in=w_in, diff_lambda=diff_lambda,
             diff_subln=diff_subln, w_attn_out=w_attn_out, ssm_d=ssm_d, w_ssm_glu=w_ssm_glu,
             conv_w=conv_w, conv_b=conv_b, conv_ln_g=conv_ln_g, conv_ln_b=conv_ln_b,
             w_conv_out=w_conv_out, b_conv_out=b_conv_out, w_merge_gate=w_merge_gate,
             b_merge_gate=b_merge_gate, w_out=w_out, router_w=router_w, router_b=router_b,
             moe_w_in=moe_w_in, moe_b_in=moe_b_in, moe_w_out=moe_w_out, moe_b_out=moe_b_out)

    cond8 = jnp.zeros((SUB, D), F32).at[0].set(c_ctx).at[1:1 + DEC_BATCH].set(c)
    mod = ada_all(cond8, w_ada, b_ada).reshape(DEPTH, SUB, 6, D)

    ab_re, ab_im, bb_re, bb_im = ssm_discretize(ssm_a_re, ssm_a_im, ssm_log_dt, ssm_b)
    ssm_p = ssm_weights(ab_re, ab_im, bb_re, bb_im, ssm_c)
    rope = rope_tables()

    ck = cache_k.reshape(DEC_BATCH, DEPTH, PAST, QKV_W)
    cv = cache_v.reshape(DEC_BATCH, DEPTH, PAST, QKV_W)
    st = state_ssm.reshape(DEC_BATCH, DEPTH, 2, 2, 1, SSM_G * SSM_N)
    h0_all = jnp.concatenate([jnp.zeros((N_SG - DEC_BATCH,) + st.shape[1:], F32), st], axis=0)

    x = jnp.concatenate([x_prompt.reshape(T_CTX, D), x_sample.reshape(T_LAT, D)], axis=0)
    new_k, new_v, new_s = [], [], []
    for l in range(DEPTH):
        x, k_l, v_l, s_l = trunk_layer(x, l, mod, p, ssm_p, rope, ck, cv, h0_all[:, l])
        new_k.append(k_l)
        new_v.append(v_l)
        new_s.append(s_l)
    y = final_norm(x, norm_final)
    return (y[:T_CTX].reshape(BATCH, SEQ, D), y[T_CTX:].reshape(DEC_BATCH, DEC_SEQ, D),
            jnp.stack(new_k, axis=1), jnp.stack(new_v, axis=1), jnp.stack(new_s, axis=1))
```
